```python
import math
import jax
import jax.numpy as jnp
from jax import lax
import numpy as np

D_MODEL = 1024
BATCH = 8
SEQ = 8192
DEPTH = 1

GRID_W = 64
CTX_LEN = 256
ATTN_HEADS = 8
ATTN_KV_HEADS = 2
ATTN_HEAD_DIM = 64
ATTN_WINDOW = 128
ATTN_BLOCK = 128
ROPE_BASE = 10000.0
ROPE_PAIRS = ATTN_HEAD_DIM // 4
DN_HEADS = 4
DN_HEAD_DIM = 128
DN_CONV = 5
DN_CHUNK = 64
PEER_HEADS = 8
PEER_KEYS = 128
PEER_EXPERTS = PEER_KEYS * PEER_KEYS
PEER_KEY_DIM = 128
PEER_KEY_HALF = PEER_KEY_DIM // 2
PEER_TOPK = 16
PEER_BLOCK = 128
RMS_EPS = 1e-6
L2_EPS = 1e-6
ATTN_Q = ATTN_HEADS * ATTN_HEAD_DIM
ATTN_KV = ATTN_KV_HEADS * ATTN_HEAD_DIM
DN_WIDTH = DN_HEADS * DN_HEAD_DIM
IN_SPLITS = (ATTN_Q, ATTN_KV, ATTN_KV, 3 * DN_WIDTH, DN_WIDTH, DN_HEADS, DN_HEADS, DN_HEADS, DN_HEADS, 2 * D_MODEL)
IN_COLS = sum(IN_SPLITS)
F32 = jnp.float32

kernel_name = 'hybrid_diffusion_swa_deltanet_peer'


def rmsnorm(x, gain):
    x32 = x.astype(F32)
    y = x32 * lax.rsqrt(jnp.mean(x32 * x32, axis=-1, keepdims=True) + RMS_EPS)
    return (y * gain.astype(F32)).astype(x.dtype)


def l2norm(x):
    x32 = x.astype(F32)
    return x32 * lax.rsqrt(jnp.sum(x32 * x32, axis=-1, keepdims=True) + L2_EPS)


def rope_angles(pos):
    inv_freq = ROPE_BASE ** (-jnp.arange(ROPE_PAIRS, dtype=F32) / ROPE_PAIRS)
    ang = pos.astype(F32)[:, None] * inv_freq[None, :]
    return jnp.cos(ang)[:, None, :], jnp.sin(ang)[:, None, :]


def rotate_pairs(x, cos, sin):
    x1, x2 = jnp.split(x.astype(F32), 2, axis=-1)
    return jnp.concatenate([x1 * cos - x2 * sin, x1 * sin + x2 * cos], axis=-1)


def axial_rope(x, rope):
    cos_r, sin_r, cos_c, sin_c = rope
    x_row, x_col = jnp.split(x, 2, axis=-1)
    out = jnp.concatenate([rotate_pairs(x_row, cos_r, sin_r), rotate_pairs(x_col, cos_c, sin_c)], axis=-1)
    return out.astype(x.dtype)


def short_conv(u, w):
    pad = DN_CONV // 2
    return lax.conv_general_dilated(u, w[:, None, :].astype(u.dtype), window_strides=(1,), padding=[(pad, pad)], dimension_numbers=('NWC', 'WIO', 'NWC'), feature_group_count=u.shape[-1])


def project_mixer_inputs(h, w_in, b_gate, conv_w):
    B, T, _ = h.shape
    p = h @ w_in
    offsets = np.cumsum(IN_SPLITS)[:-1].tolist()
    q_a, k_a, v_a, qkv_d, z_d, a_f, a_b, b_f, b_b, gates = jnp.split(p, offsets, axis=-1)
    qkv_d = jax.nn.silu(short_conv(qkv_d, conv_w))
    q_d, k_d, v_d = jnp.split(qkv_d, 3, axis=-1)
    g_attn, g_dn = jnp.split(jax.nn.sigmoid(gates + b_gate), 2, axis=-1)
    dn_heads = lambda t: t.reshape(B, T, DN_HEADS, DN_HEAD_DIM)
    return {
        'q_attn': q_a.reshape(B, T, ATTN_HEADS, ATTN_HEAD_DIM),
        'k_attn': k_a.reshape(B, T, ATTN_KV_HEADS, ATTN_HEAD_DIM),
        'v_attn': v_a.reshape(B, T, ATTN_KV_HEADS, ATTN_HEAD_DIM),
        'q_dn': l2norm(dn_heads(q_d)) * DN_HEAD_DIM ** -0.5,
        'k_dn': l2norm(dn_heads(k_d)),
        'v_dn': dn_heads(v_d).astype(F32),
        'z_dn': dn_heads(z_d),
        'a_f': a_f, 'a_b': a_b, 'b_f': b_f, 'b_b': b_b,
        'gate_attn': g_attn, 'gate_dn': g_dn,
    }


def windowed_attention(q, k, v, k_ctx, v_ctx, sink):
    B, S, H, hd = q.shape
    KV = k.shape[2]
    G = H // KV
    L = k_ctx.shape[1]
    nb = S // ATTN_BLOCK
    span = ATTN_BLOCK + 2 * ATTN_WINDOW
    scale = hd ** -0.5
    qb = jnp.moveaxis(q.reshape(B, nb, ATTN_BLOCK, KV, G, hd), 1, 0)
    pad = ((0, 0), (ATTN_WINDOW, ATTN_WINDOW), (0, 0), (0, 0))
    kp = jnp.pad(k, pad)
    vp = jnp.pad(v, pad)
    s_sink = jnp.broadcast_to(sink.astype(F32).reshape(1, KV, G, 1, 1), (B, KV, G, ATTN_BLOCK, 1))
    s_ctx_all = None

    def one_block(args):
        i, qi = args
        start = i * ATTN_BLOCK
        kw = lax.dynamic_slice_in_dim(kp, start, span, axis=1)
        vw = lax.dynamic_slice_in_dim(vp, start, span, axis=1)
        qpos = start + jnp.arange(ATTN_BLOCK)
        kpos = start - ATTN_WINDOW + jnp.arange(span)
        valid = (kpos[None, :] >= 0) & (kpos[None, :] < S) & (jnp.abs(qpos[:, None] - kpos[None, :]) <= ATTN_WINDOW)
        s_loc = jnp.einsum('bqkgd,bjkd->bkgqj', qi, kw, preferred_element_type=F32) * scale
        s_loc = jnp.where(valid, s_loc, -jnp.inf)
        s_ctx = jnp.einsum('bqkgd,bjkd->bkgqj', qi, k_ctx, preferred_element_type=F32) * scale
        p = jax.nn.softmax(jnp.concatenate([s_loc, s_ctx, s_sink], axis=-1), axis=-1)
        p_loc = p[..., :span].astype(v.dtype)
        p_ctx = p[..., span:span + L].astype(v.dtype)
        return jnp.einsum('bkgqj,bjkd->bqkgd', p_loc, vw) + jnp.einsum('bkgqj,bjkd->bqkgd', p_ctx, v_ctx)

    out = lax.map(one_block, (jnp.arange(nb), qb))
    return jnp.moveaxis(out, 0, 1).reshape(B, S, H * hd)


def context_attention(q, k, v, sink):
    B, L, H, hd = q.shape
    KV = k.shape[2]
    G = H // KV
    qg = q.reshape(B, L, KV, G, hd)
    s = jnp.einsum('bqkgd,bjkd->bkgqj', qg, k, preferred_element_type=F32) * hd ** -0.5
    s_sink = jnp.broadcast_to(sink.astype(F32).reshape(1, KV, G, 1, 1), (B, KV, G, L, 1))
    p = jax.nn.softmax(jnp.concatenate([s, s_sink], axis=-1), axis=-1)[..., :L]
    o = jnp.einsum('bkgqj,bjkd->bqkgd', p.astype(v.dtype), v)
    return o.reshape(B, L, H * hd)


def gated_delta_chunked(q, k, v, g, beta, state0):
    B, T, H, DK = q.shape
    DV = v.shape[-1]
    C = DN_CHUNK
    N = T // C

    def chunks(a):
        return jnp.moveaxis(a.reshape((B, N, C, H) + a.shape[3:]), 3, 1)

    qc, kc, vc = chunks(q.astype(F32)), chunks(k.astype(F32)), chunks(v.astype(F32))
    gc, bc = chunks(g.astype(F32)), chunks(beta.astype(F32))
    Gcum = jnp.cumsum(gc, axis=-1)
    incl = jnp.tril(jnp.ones((C, C), bool))
    strict = jnp.tril(jnp.ones((C, C), bool), -1)
    decay = jnp.exp(jnp.where(incl, Gcum[..., :, None] - Gcum[..., None, :], -jnp.inf))
    kb = kc * bc[..., None]
    A = jnp.where(strict, jnp.einsum('bhnid,bhnjd->bhnij', kb, kc) * decay, 0.0)
    eye = jnp.eye(C, dtype=F32)
    t_inv = lax.linalg.triangular_solve(A + eye, jnp.broadcast_to(eye, A.shape), left_side=True, lower=True, unit_diagonal=True)
    W = t_inv @ (kb * jnp.exp(Gcum)[..., None])
    U = t_inv @ (vc * bc[..., None])
    attn = jnp.einsum('bhnid,bhnjd->bhnij', qc, kc) * decay
    q_dec = qc * jnp.exp(Gcum)[..., None]
    k_tail = kc * jnp.exp(Gcum[..., -1:] - Gcum)[..., None]
    g_tot = jnp.exp(Gcum[..., -1])

    def step(S, xs):
        W_n, U_n, attn_n, qd_n, kt_n, gt_n = xs
        v_new = U_n - jnp.einsum('bhcd,bhde->bhce', W_n, S)
        o = jnp.einsum('bhcd,bhde->bhce', qd_n, S) + jnp.einsum('bhij,bhje->bhie', attn_n, v_new)
        S = S * gt_n[..., None, None] + jnp.einsum('bhcd,bhce->bhde', kt_n, v_new)
        return S, o

    xs = tuple(jnp.moveaxis(a, 2, 0) for a in (W, U, attn, q_dec, k_tail, g_tot))
    S_fin, o = lax.scan(step, state0.astype(F32), xs)
    o = o.transpose(1, 0, 3, 2, 4).reshape(B, T, H, DV)
    return o, S_fin


def decay_and_beta(a_logit, b_logit, a_log, dt_bias):
    g = -jnp.exp(a_log.astype(F32)) * jax.nn.softplus(a_logit.astype(F32) + dt_bias.astype(F32))
    return g, jax.nn.sigmoid(b_logit.astype(F32))


def bidirectional_delta(m, a_log_f, dt_bias_f, a_log_b, dt_bias_b, state_f, state_b):
    g_f, beta_f = decay_and_beta(m['a_f'], m['b_f'], a_log_f, dt_bias_f)
    g_b, beta_b = decay_and_beta(m['a_b'], m['b_b'], a_log_b, dt_bias_b)
    q, k, v = m['q_dn'], m['k_dn'], m['v_dn']
    o_f, s_f = gated_delta_chunked(q, k, v, g_f, beta_f, state_f)
    flip = lambda t: jnp.flip(t, axis=1)
    o_b, s_b = gated_delta_chunked(flip(q), flip(k), flip(v), flip(g_b), flip(beta_b), state_b)
    return o_f + flip(o_b), s_f, s_b


def gated_head_norm(o, z, gain):
    B, T = o.shape[:2]
    y = o * lax.rsqrt(jnp.mean(o * o, axis=-1, keepdims=True) + RMS_EPS) * gain.astype(F32)
    y = y * jax.nn.silu(z.astype(F32))
    return y.reshape(B, T, DN_WIDTH).astype(z.dtype)


def merge_branches(o_attn, o_dn, m, w_br_attn, w_br_dn, w_out):
    y = m['gate_attn'] * (o_attn @ w_br_attn) + m['gate_dn'] * (o_dn @ w_br_dn)
    return y @ w_out


def peer_ffn(h, w_q, keys, u_table, v_table):
    shape = h.shape
    blocks = h.reshape(-1, PEER_BLOCK, shape[-1])

    def block(hb):
        q = (hb @ w_q).reshape(PEER_BLOCK, PEER_HEADS, 2, PEER_KEY_HALF)
        s = jnp.einsum('thpd,hpkd->thpk', q, keys, preferred_element_type=F32)
        half_s, half_i = lax.top_k(s, PEER_TOPK)
        cand = half_s[:, :, 0, :, None] + half_s[:, :, 1, None, :]
        best_s, best_c = lax.top_k(cand.reshape(PEER_BLOCK, PEER_HEADS, PEER_TOPK * PEER_TOPK), PEER_TOPK)
        i1 = jnp.take_along_axis(half_i[:, :, 0], best_c // PEER_TOPK, axis=-1)
        i2 = jnp.take_along_axis(half_i[:, :, 1], best_c % PEER_TOPK, axis=-1)
        experts = i1 * PEER_KEYS + i2
        gate = jax.nn.softmax(best_s, axis=-1)
        act = jax.nn.gelu(jnp.einsum('td,thkd->thk', hb, u_table[experts], preferred_element_type=F32))
        return jnp.einsum('thk,thkd->td', (gate * act).astype(hb.dtype), v_table[experts])

    return lax.map(block, blocks).reshape(shape)


def setup_inputs(seed: int = 0) -> dict:
    key = jax.random.key(seed)
    ks = jax.random.split(key, 26)
    D = D_MODEL
    L = DEPTH
    nrm = lambda k, shape, scale: jax.random.normal(k, shape, F32) * scale

    def a_log(k):
        return jnp.log(jax.random.uniform(k, (L, DN_HEADS), F32, 1.0, 16.0))

    def dt_bias(k):
        dt = jnp.exp(jax.random.uniform(k, (L, DN_HEADS), F32, math.log(1e-3), math.log(1e-1)))
        return dt + jnp.log(-jnp.expm1(-dt))

    return {
        'x': nrm(ks[0], (BATCH, SEQ, D), 1.0),
        'c': nrm(ks[1], (BATCH, D), 1.0),
        'ctx': nrm(ks[2], (BATCH, CTX_LEN, D), 1.0),
        'c_ctx': nrm(ks[3], (D,), 1.0),
        'w_ada': nrm(ks[4], (L, D, 6 * D), 0.5 * D ** -0.5),
        'b_ada': nrm(ks[5], (L, 6 * D), 0.01),
        'norm_mix': 1.0 + nrm(ks[6], (L, D), 0.01),
        'norm_ffn': 1.0 + nrm(ks[7], (L, D), 0.01),
        'w_in': nrm(ks[8], (L, D, IN_COLS), D ** -0.5),
        'b_gate': nrm(ks[9], (L, 2 * D), 0.01),
        'attn_sink': nrm(ks[10], (L, ATTN_HEADS), 0.5),
        'dn_conv': nrm(ks[11], (L, DN_CONV, 3 * DN_WIDTH), DN_CONV ** -0.5),
        'dn_a_log_f': a_log(ks[12]),
        'dn_dt_bias_f': dt_bias(ks[13]),
        'dn_a_log_b': a_log(ks[14]),
        'dn_dt_bias_b': dt_bias(ks[15]),
        'dn_norm': 1.0 + nrm(ks[16], (L, DN_HEAD_DIM), 0.01),
        'w_br_attn': nrm(ks[17], (L, ATTN_Q, D), ATTN_Q ** -0.5),
        'w_br_dn': nrm(ks[18], (L, DN_WIDTH, D), DN_WIDTH ** -0.5),
        'w_out': nrm(ks[19], (L, D, D), D ** -0.5),
        'peer_wq': nrm(ks[20], (L, D, PEER_HEADS * PEER_KEY_DIM), D ** -0.5),
        'peer_keys': nrm(ks[21], (L, PEER_HEADS, 2, PEER_KEYS, PEER_KEY_HALF), PEER_KEY_HALF ** -0.5),
        'peer_u': nrm(ks[22], (L, PEER_EXPERTS, D), D ** -0.5),
        'peer_v': nrm(ks[23], (L, PEER_EXPERTS, D), PEER_HEADS ** -0.5),
        'final_norm': 1.0 + nrm(ks[24], (D,), 0.01),
    }


def reference(x, c, ctx, c_ctx, w_ada, b_ada, norm_mix, norm_ffn, w_in, b_gate, attn_sink, dn_conv,
              dn_a_log_f, dn_dt_bias_f, dn_a_log_b, dn_dt_bias_b, dn_norm, w_br_attn, w_br_dn, w_out,
              peer_wq, peer_keys, peer_u, peer_v, final_norm):
    B, S, D = x.shape
    rows = S // GRID_W
    row_pos = jnp.repeat(jnp.arange(rows, dtype=jnp.int32), GRID_W)
    col_pos = jnp.tile(jnp.arange(GRID_W, dtype=jnp.int32), rows)
    rope = (*rope_angles(row_pos), *rope_angles(col_pos))
    zero_state = jnp.zeros((B, DN_HEADS, DN_HEAD_DIM, DN_HEAD_DIM), F32)

    for layer in range(DEPTH):
        mod = jax.nn.silu(c) @ w_ada[layer] + b_ada[layer]
        sh1, sc1, gt1, sh2, sc2, gt2 = [t[:, None, :] for t in jnp.split(mod, 6, axis=-1)]
        cmod = jax.nn.silu(c_ctx) @ w_ada[layer] + b_ada[layer]
        csh1, csc1, cgt1, csh2, csc2, cgt2 = jnp.split(cmod, 6, axis=-1)

        hc = rmsnorm(ctx, norm_mix[layer]) * (1.0 + csc1) + csh1
        mc = project_mixer_inputs(hc, w_in[layer], b_gate[layer], dn_conv[layer])
        oc_dn, st_f, st_b = bidirectional_delta(mc, dn_a_log_f[layer], dn_dt_bias_f[layer], dn_a_log_b[layer], dn_dt_bias_b[layer], zero_state, zero_state)

        h = rmsnorm(x, norm_mix[layer]) * (1.0 + sc1) + sh1
        m = project_mixer_inputs(h, w_in[layer], b_gate[layer], dn_conv[layer])
        q_lat = axial_rope(m['q_attn'], rope)
        k_lat = axial_rope(m['k_attn'], rope)
        o_attn = windowed_attention(q_lat, k_lat, m['v_attn'], mc['k_attn'], mc['v_attn'], attn_sink[layer])
        o_dn, _, _ = bidirectional_delta(m, dn_a_log_f[layer], dn_dt_bias_f[layer], dn_a_log_b[layer], dn_dt_bias_b[layer], st_f, st_b)
        o_dn = gated_head_norm(o_dn, m['z_dn'], dn_norm[layer])
        x = x + gt1 * merge_branches(o_attn, o_dn, m, w_br_attn[layer], w_br_dn[layer], w_out[layer])

        h2 = rmsnorm(x, norm_ffn[layer]) * (1.0 + sc2) + sh2
        x = x + gt2 * peer_ffn(h2, peer_wq[layer], peer_keys[layer], peer_u[layer], peer_v[layer])

        if layer + 1 < DEPTH:
            oc_attn = context_attention(mc['q_attn'], mc['k_attn'], mc['v_attn'], attn_sink[layer])
            oc_dn_g = gated_head_norm(oc_dn, mc['z_dn'], dn_norm[layer])
            ctx = ctx + cgt1 * merge_branches(oc_attn, oc_dn_g, mc, w_br_attn[layer], w_br_dn[layer], w_out[layer])
            hc2 = rmsnorm(ctx, norm_ffn[layer]) * (1.0 + csc2) + csh2
            ctx = ctx + cgt2 * peer_ffn(hc2, peer_wq[layer], peer_keys[layer], peer_u[layer], peer_v[layer])

    return rmsnorm(x, final_norm)
```

```python
import functools
import math

import jax
import jax.numpy as jnp
from jax import lax
from jax.experimental import pallas as pl
from jax.experimental.pallas import tpu as pltpu

F32 = jnp.float32
BF16 = jnp.bfloat16
HIGHEST = lax.Precision.HIGHEST

GRID_W = 64
ATTN_HEADS = 8
ATTN_KV_HEADS = 2
ATTN_HEAD_DIM = 64
ATTN_WINDOW = 128
ATTN_BLOCK = 128
ROPE_BASE = 10000.0
ROPE_PAIRS = ATTN_HEAD_DIM // 4
DN_HEADS = 4
DN_HEAD_DIM = 128
DN_CONV = 5
DN_CHUNK = 64
PEER_HEADS = 8
PEER_KEYS = 128
PEER_KEY_HALF = 64
PEER_TOPK = 16
RMS_EPS = 1e-6
L2_EPS = 1e-6
ATTN_Q = ATTN_HEADS * ATTN_HEAD_DIM
ATTN_KV = ATTN_KV_HEADS * ATTN_HEAD_DIM
DN_WIDTH = DN_HEADS * DN_HEAD_DIM

LANES = 128
SUBLANES = 8
VMEM_LIMIT_BYTES = 56 * 1024 * 1024

MOD_ROWS = 16


def _dot(a, b):
    return jnp.dot(a.astype(BF16), b.astype(BF16), preferred_element_type=F32)


def _dot_nt(a, b):
    return lax.dot_general(a.astype(BF16), b.astype(BF16), (((1,), (1,)), ((), ())), preferred_element_type=F32)


def _dot_tn(a, b):
    return lax.dot_general(a.astype(BF16), b.astype(BF16), (((0,), (0,)), ((), ())), preferred_element_type=F32)


def _dot_hi(a, b):
    return jnp.dot(a, b, precision=HIGHEST, preferred_element_type=F32)


def _silu(x):
    return x * jax.nn.sigmoid(x)


def _softplus(x):
    return jnp.maximum(x, 0.0) + jnp.log1p(jnp.exp(-jnp.abs(x)))


def _params(*sem):
    return pltpu.CompilerParams(dimension_semantics=sem, vmem_limit_bytes=VMEM_LIMIT_BYTES)


def _ada_kernel(c_ref, w_ref, b_ref, o_ref):
    o_ref[...] = _dot_hi(_silu(c_ref[...]), w_ref[...]) + b_ref[...]


def _adaln(cc, w_ada, b_ada):
    d = cc.shape[1]
    n = w_ada.shape[1]
    tn = 512
    return pl.pallas_call(
        _ada_kernel,
        grid=(n // tn,),
        in_specs=[pl.BlockSpec((MOD_ROWS, d), lambda j: (0, 0)),
                  pl.BlockSpec((d, tn), lambda j: (0, j)),
                  pl.BlockSpec((1, tn), lambda j: (0, j))],
        out_specs=pl.BlockSpec((MOD_ROWS, tn), lambda j: (0, j)),
        out_shape=jax.ShapeDtypeStruct((MOD_ROWS, n), F32),
        compiler_params=_params("parallel"),
        name="adaln",
    )(cc, w_ada, b_ada.reshape(1, n))


def _rope(x, cos, sin):
    lane = lax.broadcasted_iota(jnp.int32, x.shape, 1)
    partner = jnp.where((lane % 32) < 16, pltpu.roll(x, LANES - 16, 1), pltpu.roll(x, 16, 1))
    return x * cos + partner * sin


def _inproj_kernel(*refs, rope):
    if rope:
        (x_ref, sh_ref, sc_ref, gain_ref, wa_ref, wd_ref, wzl_ref, wg_ref, bg_ref, cos_ref, sin_ref,
         qa_ref, ka_ref, va_ref, qkvd_ref, z_ref, lc_ref, ga_ref, gd_ref) = refs
    else:
        (x_ref, sh_ref, sc_ref, gain_ref, wa_ref, wd_ref, wzl_ref, wg_ref, bg_ref,
         qa_ref, ka_ref, va_ref, qkvd_ref, z_ref, lc_ref, ga_ref, gd_ref) = refs
    x = x_ref[...]
    y = x * lax.rsqrt(jnp.mean(x * x, axis=-1, keepdims=True) + RMS_EPS) * gain_ref[...]
    hb = (y * (1.0 + sc_ref[0]) + sh_ref[0]).astype(BF16)

    pa = jnp.dot(hb, wa_ref[...], preferred_element_type=F32)
    q = pa[:, :ATTN_Q] * (ATTN_HEAD_DIM ** -0.5)
    k = pa[:, ATTN_Q:ATTN_Q + ATTN_KV]
    if rope:
        cos = cos_ref[...]
        sin = sin_ref[...]
        for j in range(ATTN_Q // LANES):
            qa_ref[:, j * LANES:(j + 1) * LANES] = _rope(q[:, j * LANES:(j + 1) * LANES], cos, sin).astype(BF16)
        ka_ref[...] = _rope(k, cos, sin).astype(BF16)
    else:
        qa_ref[...] = q.astype(BF16)
        ka_ref[...] = k.astype(BF16)
    va_ref[...] = pa[:, ATTN_Q + ATTN_KV:].astype(BF16)

    qkvd_ref[...] = jnp.dot(hb, wd_ref[...], preferred_element_type=F32)
    pzl = jnp.dot(hb, wzl_ref[...], preferred_element_type=F32)
    z_ref[...] = pzl[:, :DN_WIDTH]
    lc_ref[...] = pzl[:, DN_WIDTH:]
    gates = jax.nn.sigmoid(jnp.dot(hb, wg_ref[...], preferred_element_type=F32) + bg_ref[...])
    d = ga_ref.shape[1]
    ga_ref[...] = gates[:, :d]
    gd_ref[...] = gates[:, d:]


def _inproj(xf, sh, sc, mod_index, gain, wa, wd, wzl, wg, bg, cos=None, sin=None, *, tb, pos_blocks=1):
    t, d = xf.shape
    rope = cos is not None
    row = lambda i: (i, 0)
    const = lambda i: (0, 0)
    mod = lambda i: (mod_index(i), 0, 0)
    in_specs = [pl.BlockSpec((tb, d), row),
                pl.BlockSpec((1, 1, d), mod), pl.BlockSpec((1, 1, d), mod),
                pl.BlockSpec((1, d), const),
                pl.BlockSpec(wa.shape, const), pl.BlockSpec(wd.shape, const),
                pl.BlockSpec(wzl.shape, const), pl.BlockSpec(wg.shape, const), pl.BlockSpec(bg.shape, const)]
    args = [xf, sh, sc, gain, wa, wd, wzl, wg, bg]
    if rope:
        pos = lambda i: (i % pos_blocks, 0)
        in_specs += [pl.BlockSpec((tb, LANES), pos), pl.BlockSpec((tb, LANES), pos)]
        args += [cos, sin]
    widths = [(ATTN_Q, BF16), (ATTN_KV, BF16), (ATTN_KV, BF16), (3 * DN_WIDTH, F32), (DN_WIDTH, F32),
              (4 * DN_HEADS, F32), (d, F32), (d, F32)]
    return pl.pallas_call(
        functools.partial(_inproj_kernel, rope=rope),
        grid=(t // tb,),
        in_specs=in_specs,
        out_specs=[pl.BlockSpec((tb, w), row) for w, _ in widths],
        out_shape=[jax.ShapeDtypeStruct((t, w), dt) for w, dt in widths],
        compiler_params=_params("parallel"),
        name="inproj_rope" if rope else "inproj",
    )(*args)


def _l2norm(x):
    return x * lax.rsqrt(jnp.sum(x * x, axis=-1, keepdims=True) + L2_EPS)


def _delta_direction(ext_ref, lc_ref, convw_ref, alog_ref, dtb_ref, state_ref, o_ref, *, reverse):
    c = DN_CHUNK
    halo = SUBLANES - DN_CONV // 2
    acc = ext_ref[pl.ds(halo, c), :] * convw_ref[0:1, :]
    for tap in range(1, DN_CONV):
        acc = acc + ext_ref[pl.ds(halo + tap, c), :] * convw_ref[tap:tap + 1, :]
    u = _silu(acc)

    lc = lc_ref[...]
    c0 = DN_HEADS if reverse else 0
    g = -jnp.exp(alog_ref[:, c0:c0 + DN_HEADS]) * _softplus(lc[:, c0:c0 + DN_HEADS] + dtb_ref[:, c0:c0 + DN_HEADS])
    beta = jax.nn.sigmoid(lc[:, 2 * DN_HEADS + c0:3 * DN_HEADS + c0])

    row = lax.broadcasted_iota(jnp.int32, (c, c), 0)
    col = lax.broadcasted_iota(jnp.int32, (c, c), 1)
    incl = (row <= col) if reverse else (row >= col)
    strict = (row < col) if reverse else (row > col)
    incl_f = incl.astype(F32)
    incl_t_f = ((row >= col) if reverse else (row <= col)).astype(F32)
    ones = jnp.ones((c, c), F32)
    eye = (row == col).astype(F32)
    last = 0 if reverse else c - 1

    for h in range(DN_HEADS):
        gb = jnp.broadcast_to(g[:, h:h + 1], (c, c))
        cum_col = _dot_hi(incl_f, gb)
        cum_row = _dot_hi(ones, gb * incl_t_f)
        decay = jnp.exp(jnp.where(incl, cum_col - cum_row, -jnp.inf))
        gcol = cum_col[:, 0:1]
        glast = cum_col[last:last + 1, 0:1]

        q = _l2norm(u[:, h * DN_HEAD_DIM:(h + 1) * DN_HEAD_DIM]) * (DN_HEAD_DIM ** -0.5)
        k = _l2norm(u[:, DN_WIDTH + h * DN_HEAD_DIM:DN_WIDTH + (h + 1) * DN_HEAD_DIM])
        v = u[:, 2 * DN_WIDTH + h * DN_HEAD_DIM:2 * DN_WIDTH + (h + 1) * DN_HEAD_DIM]
        bcol = beta[:, h:h + 1]
        kb = k * bcol

        a = jnp.where(strict, _dot_nt(kb, k) * decay, 0.0)
        p = -a
        t_inv = eye + p
        for _ in range(int(math.log2(c)) - 1):
            p = _dot_hi(p, p)
            t_inv = t_inv + _dot_hi(t_inv, p)

        eg = jnp.exp(gcol)
        w = _dot(t_inv, kb * eg)
        uu = _dot(t_inv, v * bcol)
        attn = _dot_nt(q, k) * decay
        q_dec = q * eg
        k_tail = k * jnp.exp(glast - gcol)

        s = state_ref[h]
        v_new = uu - _dot(w, s)
        o_ref[:, h * DN_HEAD_DIM:(h + 1) * DN_HEAD_DIM] = _dot(q_dec, s) + _dot(attn, v_new)
        state_ref[h] = s * jnp.exp(glast) + _dot_tn(k_tail, v_new)


def _delta_kernel(cur_f, prev_f, next_f, cur_b, prev_b, next_b, lc_f, lc_b, convw, alog, dtb, s0f, s0b,
                  of_ref, ob_ref, sf_ref, sb_ref, ext_f, ext_b, state_f, state_b, *, n_chunks):
    n = pl.program_id(1)

    @pl.when(n == 0)
    def _():
        state_f[...] = s0f[0]
        state_b[...] = s0b[0]

    c = DN_CHUNK
    for ext, prev, cur, nxt, chunk in ((ext_f, prev_f, cur_f, next_f, n), (ext_b, prev_b, cur_b, next_b, n_chunks - 1 - n)):
        ext[0:SUBLANES, :] = prev[...] * (chunk > 0).astype(F32)
        ext[SUBLANES:SUBLANES + c, :] = cur[...]
        ext[SUBLANES + c:, :] = nxt[...] * (chunk < n_chunks - 1).astype(F32)

    _delta_direction(ext_f, lc_f, convw, alog, dtb, state_f, of_ref, reverse=False)
    _delta_direction(ext_b, lc_b, convw, alog, dtb, state_b, ob_ref, reverse=True)

    @pl.when(n == n_chunks - 1)
    def _():
        sf_ref[0] = state_f[...]
        sb_ref[0] = state_b[...]


def _delta(qkvd, lc, convw, alog, dtb, s0f, s0b, *, batch):
    t, wd = qkvd.shape
    c = DN_CHUNK
    n_chunks = t // batch // c
    sub_per_chunk = c // SUBLANES
    last_sub = t // SUBLANES - 1
    fwd = lambda b, n: b * n_chunks + n
    bwd = lambda b, n: b * n_chunks + n_chunks - 1 - n

    def chunk_specs(idx):
        return [pl.BlockSpec((c, wd), lambda b, n: (idx(b, n), 0)),
                pl.BlockSpec((SUBLANES, wd), lambda b, n: (jnp.maximum(idx(b, n) * sub_per_chunk - 1, 0), 0)),
                pl.BlockSpec((SUBLANES, wd), lambda b, n: (jnp.minimum((idx(b, n) + 1) * sub_per_chunk, last_sub), 0))]

    const = lambda b, n: (0, 0)
    state_spec = pl.BlockSpec((1, DN_HEADS, DN_HEAD_DIM, DN_HEAD_DIM), lambda b, n: (b, 0, 0, 0))
    state_shape = jax.ShapeDtypeStruct((batch, DN_HEADS, DN_HEAD_DIM, DN_HEAD_DIM), F32)
    return pl.pallas_call(
        functools.partial(_delta_kernel, n_chunks=n_chunks),
        grid=(batch, n_chunks),
        in_specs=chunk_specs(fwd) + chunk_specs(bwd) + [
            pl.BlockSpec((c, lc.shape[1]), lambda b, n: (fwd(b, n), 0)),
            pl.BlockSpec((c, lc.shape[1]), lambda b, n: (bwd(b, n), 0)),
            pl.BlockSpec(convw.shape, const), pl.BlockSpec(alog.shape, const), pl.BlockSpec(dtb.shape, const),
            state_spec, state_spec],
        out_specs=[pl.BlockSpec((c, DN_WIDTH), lambda b, n: (fwd(b, n), 0)),
                   pl.BlockSpec((c, DN_WIDTH), lambda b, n: (bwd(b, n), 0)),
                   state_spec, state_spec],
        out_shape=[jax.ShapeDtypeStruct((t, DN_WIDTH), F32), jax.ShapeDtypeStruct((t, DN_WIDTH), F32),
                   state_shape, state_shape],
        scratch_shapes=[pltpu.VMEM((c + 2 * SUBLANES, wd), F32), pltpu.VMEM((c + 2 * SUBLANES, wd), F32),
                        pltpu.VMEM((DN_HEADS, DN_HEAD_DIM, DN_HEAD_DIM), F32),
                        pltpu.VMEM((DN_HEADS, DN_HEAD_DIM, DN_HEAD_DIM), F32)],
        compiler_params=_params("parallel", "arbitrary"),
        name="delta",
    )(qkvd, qkvd, qkvd, qkvd, qkvd, qkvd, lc, lc, convw, alog, dtb, s0f, s0b)


def _attn_kernel(q_ref, kp_ref, kc_ref, kn_ref, vp_ref, vc_ref, vn_ref, kx_ref, vx_ref, sink_ref, o_ref, *, n_blocks):
    i = pl.program_id(1)
    qb = ATTN_BLOCK
    span = qb + 2 * ATTN_WINDOW
    n_ctx = kx_ref.shape[0]
    qpos = lax.broadcasted_iota(jnp.int32, (qb, span + n_ctx), 0)
    j = lax.broadcasted_iota(jnp.int32, (qb, span + n_ctx), 1)
    rel = j - qpos
    local = (rel >= 0) & (rel <= 2 * ATTN_WINDOW)
    local = local & ((j >= ATTN_WINDOW) | (i > 0)) & ((j < ATTN_WINDOW + qb) | (i < n_blocks - 1))
    valid = local | (j >= span)

    group = ATTN_HEADS // ATTN_KV_HEADS
    hd = ATTN_HEAD_DIM
    for kv in range(ATTN_KV_HEADS):
        sl = slice(kv * hd, (kv + 1) * hd)
        keys = jnp.concatenate([kp_ref[:, sl], kc_ref[:, sl], kn_ref[:, sl], kx_ref[:, sl]], axis=0)
        vals = jnp.concatenate([vp_ref[:, sl], vc_ref[:, sl], vn_ref[:, sl], vx_ref[:, sl]], axis=0)
        for gi in range(group):
            h = kv * group + gi
            s = _dot_nt(q_ref[:, h * hd:(h + 1) * hd], keys)
            s = jnp.where(valid, s, -jnp.inf)
            sink = sink_ref[0:1, h:h + 1]
            m = jnp.maximum(jnp.max(s, axis=-1, keepdims=True), sink)
            p = jnp.exp(s - m)
            denom = jnp.sum(p, axis=-1, keepdims=True) + jnp.exp(sink - m)
            o_ref[:, h * hd:(h + 1) * hd] = _dot(p / denom, vals).astype(o_ref.dtype)


def _attention(qa, ka, va, kx, vx, sink, *, batch):
    t = qa.shape[0]
    qb = ATTN_BLOCK
    n_blocks = t // batch // qb
    n_ctx = kx.shape[0] // batch
    cur = lambda b, i: (b * n_blocks + i, 0)
    prev = lambda b, i: (b * n_blocks + jnp.maximum(i - 1, 0), 0)
    nxt = lambda b, i: (b * n_blocks + jnp.minimum(i + 1, n_blocks - 1), 0)
    ctx = lambda b, i: (b, 0)
    kv_specs = [pl.BlockSpec((qb, ATTN_KV), prev), pl.BlockSpec((qb, ATTN_KV), cur), pl.BlockSpec((qb, ATTN_KV), nxt)]
    return pl.pallas_call(
        functools.partial(_attn_kernel, n_blocks=n_blocks),
        grid=(batch, n_blocks),
        in_specs=[pl.BlockSpec((qb, ATTN_Q), cur)] + kv_specs + kv_specs + [
            pl.BlockSpec((n_ctx, ATTN_KV), ctx), pl.BlockSpec((n_ctx, ATTN_KV), ctx),
            pl.BlockSpec(sink.shape, lambda b, i: (0, 0))],
        out_specs=pl.BlockSpec((qb, ATTN_Q), cur),
        out_shape=jax.ShapeDtypeStruct((t, ATTN_Q), BF16),
        compiler_params=_params("parallel", "parallel"),
        name="attention",
    )(qa, ka, ka, ka, va, va, va, kx, vx, sink)


def _merge_kernel(x_ref, gt_ref, of_ref, ob_ref, z_ref, dng_ref, oa_ref, ga_ref, gd_ref, wba_ref, wbd_ref, wo_ref, x1_ref):
    o = of_ref[...] + ob_ref[...]
    z = z_ref[...]
    heads = []
    for h in range(DN_HEADS):
        sl = slice(h * DN_HEAD_DIM, (h + 1) * DN_HEAD_DIM)
        oh = o[:, sl]
        yh = oh * lax.rsqrt(jnp.mean(oh * oh, axis=-1, keepdims=True) + RMS_EPS) * dng_ref[...]
        heads.append((yh * _silu(z[:, sl])).astype(BF16))
    o_dn = jnp.concatenate(heads, axis=1)
    y = (ga_ref[...] * jnp.dot(oa_ref[...], wba_ref[...], preferred_element_type=F32)
         + gd_ref[...] * jnp.dot(o_dn, wbd_ref[...], preferred_element_type=F32))
    x1_ref[...] = x_ref[...] + gt_ref[0] * jnp.dot(y.astype(BF16), wo_ref[...], preferred_element_type=F32)


def _merge(xf, gt1, o_f, o_b, z, dn_gain, o_attn, g_attn, g_dn, wba, wbd, wo, *, tb, blocks_per_batch):
    t, d = xf.shape
    row = lambda i: (i, 0)
    const = lambda i: (0, 0)
    rows = lambda w: pl.BlockSpec((tb, w), row)
    return pl.pallas_call(
        _merge_kernel,
        grid=(t // tb,),
        in_specs=[rows(d), pl.BlockSpec((1, 1, d), lambda i: (i // blocks_per_batch, 0, 0)),
                  rows(DN_WIDTH), rows(DN_WIDTH), rows(DN_WIDTH), pl.BlockSpec(dn_gain.shape, const),
                  rows(ATTN_Q), rows(d), rows(d),
                  pl.BlockSpec(wba.shape, const), pl.BlockSpec(wbd.shape, const), pl.BlockSpec(wo.shape, const)],
        out_specs=rows(d),
        out_shape=jax.ShapeDtypeStruct((t, d), F32),
        compiler_params=_params("parallel"),
        name="merge",
    )(xf, gt1, o_f, o_b, z, dn_gain, o_attn, g_attn, g_dn, wba, wbd, wo)


def _gelu_tanh(x):
    return 0.5 * x * (1.0 + jnp.tanh(math.sqrt(2.0 / math.pi) * (x + 0.044715 * (x * x * x))))


def _top_rows(x, k):
    rows = []
    for _ in range(k):
        m = jnp.max(x, axis=0, keepdims=True)
        rows.append(m)
        x = jnp.where(x == m, -jnp.inf, x)
    return rows


def _peer_kernel(x1_ref, sh_ref, sc_ref, gt_ref, gain_ref, fgain_ref, wqt_ref, keys_ref, u_ref, vt_ref, o_ref,
                 h_scr, s1_scr, s2_scr, a_scr, b_scr, tau_scr, act_scr, wa_scr, acc_scr, *, n_tiles):
    e = pl.program_id(1)
    tile_rows = u_ref.shape[0] // PEER_KEYS

    @pl.when(e == 0)
    def _():
        x1 = x1_ref[...]
        y = x1 * lax.rsqrt(jnp.mean(x1 * x1, axis=-1, keepdims=True) + RMS_EPS) * gain_ref[...]
        hb = (y * (1.0 + sc_ref[0]) + sh_ref[0]).astype(BF16)
        h_scr[...] = hb
        qt = lax.dot_general(wqt_ref[...], hb, (((1,), (1,)), ((), ())), preferred_element_type=F32)
        for h in range(PEER_HEADS):
            halves = []
            for p in range(2):
                r0 = (2 * h + p) * PEER_KEY_HALF
                halves.append(jnp.dot(keys_ref[2 * h + p], qt[r0:r0 + PEER_KEY_HALF].astype(BF16),
                                      preferred_element_type=F32))
            s1, s2 = halves
            top1 = _top_rows(s1, PEER_TOPK)
            top2 = jnp.concatenate(_top_rows(s2, PEER_TOPK), axis=0)
            cand = jnp.concatenate([r + top2 for r in top1], axis=0)
            best = _top_rows(cand, PEER_TOPK)
            zsum = jnp.zeros_like(best[0])
            for r in best:
                zsum = zsum + jnp.exp(r - best[0])
            s1_scr[h] = s1
            s2_scr[h] = s2
            tau_scr[h] = jnp.broadcast_to(best[-1], tau_scr.shape[1:])
            a_scr[h] = jnp.exp(s1 - top1[0]) / zsum
            b_scr[h] = jnp.exp(s2 - top2[0:1])
        acc_scr[...] = jnp.zeros_like(acc_scr)

    act_scr[...] = lax.dot_general(u_ref[...], h_scr[...], (((1,), (1,)), ((), ())), preferred_element_type=F32)

    def one_row(ii, carry):
        i = e * tile_rows + ii
        r0 = pl.multiple_of(ii * PEER_KEYS, PEER_KEYS)
        w = jnp.zeros((PEER_KEYS, act_scr.shape[1]), F32)
        for h in range(PEER_HEADS):
            score = s1_scr[h, pl.ds(i, 1), :] + s2_scr[h]
            w = w + jnp.where(score >= tau_scr[h, 0:1, :], a_scr[h, pl.ds(i, 1), :] * b_scr[h], 0.0)
        wa_scr[pl.ds(r0, PEER_KEYS), :] = (w * _gelu_tanh(act_scr[pl.ds(r0, PEER_KEYS), :])).astype(BF16)
        return carry

    lax.fori_loop(0, tile_rows, one_row, 0)
    acc_scr[...] += jnp.dot(vt_ref[...], wa_scr[...], preferred_element_type=F32)

    @pl.when(e == n_tiles - 1)
    def _():
        y = x1_ref[...] + gt_ref[0] * acc_scr[...].T
        o_ref[...] = y * lax.rsqrt(jnp.mean(y * y, axis=-1, keepdims=True) + RMS_EPS) * fgain_ref[...]


def _peer(x1, sh2, sc2, gt2, gain, fgain, wqt, keys, u, vt, *, tb, et, blocks_per_batch):
    t, d = x1.shape
    n_exp = u.shape[0]
    n_tiles = n_exp // et
    row = lambda i, e: (i, 0)
    const2 = lambda i, e: (0, 0)
    mod = lambda i, e: (i // blocks_per_batch, 0, 0)
    kt = (PEER_HEADS, PEER_KEYS, tb)
    return pl.pallas_call(
        functools.partial(_peer_kernel, n_tiles=n_tiles),
        grid=(t // tb, n_tiles),
        in_specs=[pl.BlockSpec((tb, d), row),
                  pl.BlockSpec((1, 1, d), mod), pl.BlockSpec((1, 1, d), mod), pl.BlockSpec((1, 1, d), mod),
                  pl.BlockSpec((1, d), const2), pl.BlockSpec((1, d), const2),
                  pl.BlockSpec(wqt.shape, const2), pl.BlockSpec(keys.shape, lambda i, e: (0, 0, 0)),
                  pl.BlockSpec((et, d), lambda i, e: (e, 0)), pl.BlockSpec((d, et), lambda i, e: (0, e))],
        out_specs=pl.BlockSpec((tb, d), row),
        out_shape=jax.ShapeDtypeStruct((t, d), F32),
        scratch_shapes=[pltpu.VMEM((tb, d), BF16),
                        pltpu.VMEM(kt, F32), pltpu.VMEM(kt, F32), pltpu.VMEM(kt, F32), pltpu.VMEM(kt, F32),
                        pltpu.VMEM((PEER_HEADS, SUBLANES, tb), F32),
                        pltpu.VMEM((et, tb), F32), pltpu.VMEM((et, tb), BF16), pltpu.VMEM((d, tb), F32)],
        compiler_params=_params("parallel", "arbitrary"),
        name="peer",
    )(x1, sh2, sc2, gt2, gain, fgain, wqt, keys, u, vt)


def _rope_tables(seq):
    t = jnp.arange(seq, dtype=jnp.int32)
    inv_freq = ROPE_BASE ** (-jnp.arange(ROPE_PAIRS, dtype=F32) / ROPE_PAIRS)
    ang_r = (t // GRID_W).astype(F32)[:, None] * inv_freq[None, :]
    ang_c = (t % GRID_W).astype(F32)[:, None] * inv_freq[None, :]
    cos_h = jnp.concatenate([jnp.cos(ang_r), jnp.cos(ang_r), jnp.cos(ang_c), jnp.cos(ang_c)], axis=1)
    sin_h = jnp.concatenate([-jnp.sin(ang_r), jnp.sin(ang_r), -jnp.sin(ang_c), jnp.sin(ang_c)], axis=1)
    reps = LANES // ATTN_HEAD_DIM
    return jnp.tile(cos_h, (1, reps)), jnp.tile(sin_h, (1, reps))


def _pick_block(n, target):
    b = min(n, target)
    while n % b:
        b //= 2
    return b


def kernel(x, c, ctx, c_ctx, w_ada, b_ada, norm_mix, norm_ffn, w_in, b_gate, attn_sink, dn_conv, dn_a_log_f, dn_dt_bias_f, dn_a_log_b, dn_dt_bias_b, dn_norm, w_br_attn, w_br_dn, w_out, peer_wq, peer_keys, peer_u, peer_v, final_norm):
    bsz, seq, d = x.shape
    n_ctx = ctx.shape[1]
    depth = w_ada.shape[0]
    assert depth == 1, "context-stream update between layers is not implemented"
    assert bsz + 1 <= MOD_ROWS and seq % ATTN_BLOCK == 0 and n_ctx % DN_CHUNK == 0
    layer = 0
    xf = x.reshape(bsz * seq, d)
    ctxf = ctx.reshape(bsz * n_ctx, d)

    cc = jnp.zeros((MOD_ROWS, d), F32).at[:bsz].set(c).at[bsz].set(c_ctx)
    mod = _adaln(cc, w_ada[layer], b_ada[layer])
    sh1, sc1, gt1, sh2, sc2, gt2 = [m.reshape(MOD_ROWS, 1, d) for m in jnp.split(mod, 6, axis=-1)]

    w = w_in[layer].astype(BF16)
    o_qkv = ATTN_Q + 2 * ATTN_KV
    o_d = o_qkv + 3 * DN_WIDTH
    o_zl = o_d + DN_WIDTH + 4 * DN_HEADS
    wa, wd, wzl, wg = w[:, :o_qkv], w[:, o_qkv:o_d], w[:, o_d:o_zl], w[:, o_zl:]
    bg = b_gate[layer].reshape(1, -1)
    gain_mix = norm_mix[layer].reshape(1, d)

    convw = dn_conv[layer]
    alog = jnp.concatenate([dn_a_log_f[layer], dn_a_log_b[layer]]).reshape(1, 2 * DN_HEADS)
    dtb = jnp.concatenate([dn_dt_bias_f[layer], dn_dt_bias_b[layer]]).reshape(1, 2 * DN_HEADS)

    tb_c = _pick_block(bsz * n_ctx, 512)
    _, kx, vx, qkvd_c, _, lc_c, _, _ = _inproj(ctxf, sh1, sc1, lambda i: bsz, gain_mix, wa, wd, wzl, wg, bg, tb=tb_c)
    zero_state = jnp.zeros((bsz, DN_HEADS, DN_HEAD_DIM, DN_HEAD_DIM), F32)
    _, _, st_f, st_b = _delta(qkvd_c, lc_c, convw, alog, dtb, zero_state, zero_state, batch=bsz)

    tb = _pick_block(seq, 512)
    cos, sin = _rope_tables(seq)
    qa, ka, va, qkvd, z, lc, g_attn, g_dn = _inproj(
        xf, sh1, sc1, lambda i: i // (seq // tb), gain_mix, wa, wd, wzl, wg, bg, cos, sin, tb=tb, pos_blocks=seq // tb)
    o_attn = _attention(qa, ka, va, kx, vx, attn_sink[layer].reshape(1, ATTN_HEADS), batch=bsz)
    o_f, o_b, _, _ = _delta(qkvd, lc, convw, alog, dtb, st_f, st_b, batch=bsz)
    x1 = _merge(xf, gt1, o_f, o_b, z, dn_norm[layer].reshape(1, DN_HEAD_DIM), o_attn, g_attn, g_dn,
                w_br_attn[layer].astype(BF16), w_br_dn[layer].astype(BF16), w_out[layer].astype(BF16),
                tb=tb, blocks_per_batch=seq // tb)

    tb_p = _pick_block(seq, 512)
    out = _peer(x1, sh2, sc2, gt2, norm_ffn[layer].reshape(1, d), final_norm.reshape(1, d),
                peer_wq[layer].T.astype(BF16),
                peer_keys[layer].reshape(2 * PEER_HEADS, PEER_KEYS, PEER_KEY_HALF).astype(BF16),
                peer_u[layer].astype(BF16), peer_v[layer].T.astype(BF16),
                tb=tb_p, et=1024, blocks_per_batch=seq // tb_p)
    return out.reshape(bsz, seq, d)
```

```python
import functools
import math

import jax
import jax.numpy as jnp
from jax import lax
from jax.experimental import pallas as pl
from jax.experimental.pallas import tpu as pltpu

F32 = jnp.float32
BF16 = jnp.bfloat16
HIGHEST = lax.Precision.HIGHEST

GRID_W = 64
ATTN_HEADS = 8
ATTN_KV_HEADS = 2
ATTN_HEAD_DIM = 64
ATTN_WINDOW = 128
ATTN_BLOCK = 128
ROPE_BASE = 10000.0
ROPE_PAIRS = ATTN_HEAD_DIM // 4
DN_HEADS = 4
DN_HEAD_DIM = 128
DN_CONV = 5
DN_CHUNK = 64
PEER_HEADS = 8
PEER_KEYS = 128
PEER_KEY_HALF = 64
PEER_TOPK = 16
RMS_EPS = 1e-6
L2_EPS = 1e-6
ATTN_Q = ATTN_HEADS * ATTN_HEAD_DIM
ATTN_KV = ATTN_KV_HEADS * ATTN_HEAD_DIM
DN_WIDTH = DN_HEADS * DN_HEAD_DIM

LANES = 128
SUBLANES = 8
VMEM_LIMIT_BYTES = 56 * 1024 * 1024

MOD_ROWS = 16


def _dot(a, b):
    return jnp.dot(a.astype(BF16), b.astype(BF16), preferred_element_type=F32)


def _dot_nt(a, b):
    return lax.dot_general(a.astype(BF16), b.astype(BF16), (((1,), (1,)), ((), ())), preferred_element_type=F32)


def _dot_tn(a, b):
    return lax.dot_general(a.astype(BF16), b.astype(BF16), (((0,), (0,)), ((), ())), preferred_element_type=F32)


def _dot_hi(a, b):
    return jnp.dot(a, b, precision=HIGHEST, preferred_element_type=F32)


def _silu(x):
    return x * jax.nn.sigmoid(x)


def _softplus(x):
    return jnp.maximum(x, 0.0) + jnp.log1p(jnp.exp(-jnp.abs(x)))


def _params(*sem):
    return pltpu.CompilerParams(dimension_semantics=sem, vmem_limit_bytes=VMEM_LIMIT_BYTES)


def _ada_kernel(c_ref, w_ref, b_ref, o_ref):
    o_ref[...] = _dot_hi(_silu(c_ref[...]), w_ref[...]) + b_ref[...]


def _adaln(cc, w_ada, b_ada):
    d = cc.shape[1]
    n = w_ada.shape[1]
    tn = 512
    return pl.pallas_call(
        _ada_kernel,
        grid=(n // tn,),
        in_specs=[pl.BlockSpec((MOD_ROWS, d), lambda j: (0, 0)),
                  pl.BlockSpec((d, tn), lambda j: (0, j)),
                  pl.BlockSpec((1, tn), lambda j: (0, j))],
        out_specs=pl.BlockSpec((MOD_ROWS, tn), lambda j: (0, j)),
        out_shape=jax.ShapeDtypeStruct((MOD_ROWS, n), F32),
        compiler_params=_params("parallel"),
        name="adaln",
    )(cc, w_ada, b_ada.reshape(1, n))


def _rope(x, cos, sin):
    lane = lax.broadcasted_iota(jnp.int32, x.shape, 1)
    partner = jnp.where((lane % 32) < 16, pltpu.roll(x, LANES - 16, 1), pltpu.roll(x, 16, 1))
    return x * cos + partner * sin


def _inproj_kernel(*refs, rope):
    if rope:
        (x_ref, sh_ref, sc_ref, gain_ref, wa_ref, wd_ref, wzl_ref, wg_ref, bg_ref, cos_ref, sin_ref,
         qa_ref, ka_ref, va_ref, qkvd_ref, z_ref, lc_ref, ga_ref, gd_ref) = refs
    else:
        (x_ref, sh_ref, sc_ref, gain_ref, wa_ref, wd_ref, wzl_ref, wg_ref, bg_ref,
         qa_ref, ka_ref, va_ref, qkvd_ref, z_ref, lc_ref, ga_ref, gd_ref) = refs
    x = x_ref[...]
    y = x * lax.rsqrt(jnp.mean(x * x, axis=-1, keepdims=True) + RMS_EPS) * gain_ref[...]
    hb = (y * (1.0 + sc_ref[0]) + sh_ref[0]).astype(BF16)

    pa = jnp.dot(hb, wa_ref[...], preferred_element_type=F32)
    q = pa[:, :ATTN_Q] * (ATTN_HEAD_DIM ** -0.5)
    k = pa[:, ATTN_Q:ATTN_Q + ATTN_KV]
    if rope:
        cos = cos_ref[...]
        sin = sin_ref[...]
        for j in range(ATTN_Q // LANES):
            qa_ref[:, j * LANES:(j + 1) * LANES] = _rope(q[:, j * LANES:(j + 1) * LANES], cos, sin).astype(BF16)
        ka_ref[...] = _rope(k, cos, sin).astype(BF16)
    else:
        qa_ref[...] = q.astype(BF16)
        ka_ref[...] = k.astype(BF16)
    va_ref[...] = pa[:, ATTN_Q + ATTN_KV:].astype(BF16)

    qkvd_ref[...] = jnp.dot(hb, wd_ref[...], preferred_element_type=F32)
    pzl = jnp.dot(hb, wzl_ref[...], preferred_element_type=F32)
    z_ref[...] = pzl[:, :DN_WIDTH]
    lc_ref[...] = pzl[:, DN_WIDTH:]
    gates = jax.nn.sigmoid(jnp.dot(hb, wg_ref[...], preferred_element_type=F32) + bg_ref[...])
    d = ga_ref.shape[1]
    ga_ref[...] = gates[:, :d]
    gd_ref[...] = gates[:, d:]


def _inproj(xf, sh, sc, mod_index, gain, wa, wd, wzl, wg, bg, cos=None, sin=None, *, tb, pos_blocks=1):
    t, d = xf.shape
    rope = cos is not None
    row = lambda i: (i, 0)
    const = lambda i: (0, 0)
    mod = lambda i: (mod_index(i), 0, 0)
    in_specs = [pl.BlockSpec((tb, d), row),
                pl.BlockSpec((1, 1, d), mod), pl.BlockSpec((1, 1, d), mod),
                pl.BlockSpec((1, d), const),
                pl.BlockSpec(wa.shape, const), pl.BlockSpec(wd.shape, const),
                pl.BlockSpec(wzl.shape, const), pl.BlockSpec(wg.shape, const), pl.BlockSpec(bg.shape, const)]
    args = [xf, sh, sc, gain, wa, wd, wzl, wg, bg]
    if rope:
        pos = lambda i: (i % pos_blocks, 0)
        in_specs += [pl.BlockSpec((tb, LANES), pos), pl.BlockSpec((tb, LANES), pos)]
        args += [cos, sin]
    widths = [(ATTN_Q, BF16), (ATTN_KV, BF16), (ATTN_KV, BF16), (3 * DN_WIDTH, F32), (DN_WIDTH, F32),
              (4 * DN_HEADS, F32), (d, F32), (d, F32)]
    return pl.pallas_call(
        functools.partial(_inproj_kernel, rope=rope),
        grid=(t // tb,),
        in_specs=in_specs,
        out_specs=[pl.BlockSpec((tb, w), row) for w, _ in widths],
        out_shape=[jax.ShapeDtypeStruct((t, w), dt) for w, dt in widths],
        compiler_params=_params("parallel"),
        name="inproj_rope" if rope else "inproj",
    )(*args)


def _l2norm(x):
    return x * lax.rsqrt(jnp.sum(x * x, axis=-1, keepdims=True) + L2_EPS)


def _split_bf16(x):
    hi = x.astype(BF16).astype(F32)
    return hi, x - hi


def _dot_3pass(a, b):
    ah, al = _split_bf16(a)
    bh, bl = _split_bf16(b)
    return _dot(jnp.concatenate([ah, ah, al], axis=1), jnp.concatenate([bh, bl, bh], axis=0))


PREP_CHUNKS = 2
SCAN_CHUNKS = 2


def _delta_prep_kernel(cur_ref, prev_ref, next_ref, lc_ref, convw_ref, alog_ref, dtb_ref,
                       wq_f, u_f, ktt_f, att_f, gt_f, wq_b, u_b, ktt_b, att_b, gt_b, ext_ref, *, steps_per_seq):
    j = pl.program_id(0)
    c = DN_CHUNK
    rows = PREP_CHUNKS * c
    halo = SUBLANES - DN_CONV // 2
    first = (j % steps_per_seq) == 0
    last_step = (j % steps_per_seq) == steps_per_seq - 1
    ext_ref[0:SUBLANES, :] = prev_ref[...] * (1.0 - first.astype(F32))
    ext_ref[SUBLANES:SUBLANES + rows, :] = cur_ref[...]
    ext_ref[SUBLANES + rows:, :] = next_ref[...] * (1.0 - last_step.astype(F32))
    acc = ext_ref[pl.ds(halo, rows), :] * convw_ref[0:1, :]
    for tap in range(1, DN_CONV):
        acc = acc + ext_ref[pl.ds(halo + tap, rows), :] * convw_ref[tap:tap + 1, :]
    u = _silu(acc)

    lc = lc_ref[...]
    nd = 2 * DN_HEADS
    g_all = -jnp.exp(alog_ref[...]) * _softplus(lc[:, :nd] + dtb_ref[...])
    beta_all = jax.nn.sigmoid(lc[:, nd:])

    row = lax.broadcasted_iota(jnp.int32, (c, c), 0)
    col = lax.broadcasted_iota(jnp.int32, (c, c), 1)
    eye = (row == col).astype(F32)
    lower_f = (row >= col).astype(F32)
    upper_f = (row <= col).astype(F32)
    is_fwd_col = lax.broadcasted_iota(jnp.int32, (c, nd), 1) < DN_HEADS

    out_refs = ((wq_f, u_f, ktt_f, att_f, gt_f), (wq_b, u_b, ktt_b, att_b, gt_b))
    chains = []
    for ci in range(PREP_CHUNKS):
        r0 = ci * c
        g = g_all[r0:r0 + c]
        cum = jnp.where(is_fwd_col, _dot_hi(lower_f, g), _dot_hi(upper_f, g))
        cum_t = cum.T
        for h in range(DN_HEADS):
            hs = slice(h * DN_HEAD_DIM, (h + 1) * DN_HEAD_DIM)
            q = _l2norm(u[r0:r0 + c, hs]) * (DN_HEAD_DIM ** -0.5)
            k = _l2norm(u[r0:r0 + c, DN_WIDTH + h * DN_HEAD_DIM:DN_WIDTH + (h + 1) * DN_HEAD_DIM])
            v = u[r0:r0 + c, 2 * DN_WIDTH + h * DN_HEAD_DIM:2 * DN_WIDTH + (h + 1) * DN_HEAD_DIM]
            for d in range(2):
                reverse = d == 1
                cc = d * DN_HEADS + h
                gcol = cum[:, cc:cc + 1]
                incl = (row <= col) if reverse else (row >= col)
                last = 0 if reverse else c - 1
                chains.append(dict(
                    ci=ci, d=d, h=h, q=q, k=k, v=v, gcol=gcol, glast=gcol[last:last + 1, :],
                    bcol=beta_all[r0:r0 + c, cc:cc + 1],
                    decay=jnp.exp(jnp.where(incl, gcol - cum_t[cc:cc + 1, :], -jnp.inf)),
                    strict=(row < col) if reverse else (row > col)))

    for ch in chains:
        ch["kb"] = ch["k"] * ch["bcol"]
        scores = _dot_nt(jnp.concatenate([ch["kb"], ch["q"]], axis=0), ch["k"])
        ch["x"] = -jnp.where(ch["strict"], scores[:c] * ch["decay"], 0.0)
        ch["attn"] = scores[c:] * ch["decay"]
    for ch in chains:
        ch["t"] = eye + ch["x"]
        ch["qpow"] = _dot_3pass(ch["x"], ch["x"])
    levels = int(math.log2(c)) - 1
    for lvl in range(levels):
        for ch in chains:
            if lvl < levels - 1:
                both = _dot_3pass(ch["qpow"], jnp.concatenate([ch["t"], ch["qpow"]], axis=1))
                ch["t"] = ch["t"] + both[:, :c]
                ch["qpow"] = both[:, c:]
            else:
                ch["t"] = ch["t"] + _dot_3pass(ch["qpow"], ch["t"])
    for ch in chains:
        eg = jnp.exp(ch["gcol"])
        ch["wu"] = _dot(ch["t"], jnp.concatenate([ch["kb"] * eg, ch["v"] * ch["bcol"]], axis=1))
        ch["qd"] = ch["q"] * eg
    for ch in chains:
        wq_ref, u_ref, ktt_ref, att_ref, gt_ref = out_refs[ch["d"]]
        ci, h = ch["ci"], ch["h"]
        idx = ci * DN_HEADS + h
        wq_ref[idx, 0:c, :] = ch["wu"][:, :DN_HEAD_DIM].astype(BF16)
        wq_ref[idx, c:, :] = ch["qd"].astype(BF16)
        u_ref[ci * c:(ci + 1) * c, h * DN_HEAD_DIM:(h + 1) * DN_HEAD_DIM] = ch["wu"][:, DN_HEAD_DIM:]
        k_tail = ch["k"] * jnp.exp(ch["glast"] - ch["gcol"])
        ktt_ref[ci * DN_HEAD_DIM:(ci + 1) * DN_HEAD_DIM, h * c:(h + 1) * c] = k_tail.T.astype(BF16)
        att_ref[ci * c:(ci + 1) * c, h * c:(h + 1) * c] = ch["attn"].astype(BF16)
        gt_ref[idx] = jnp.broadcast_to(jnp.exp(ch["glast"]), gt_ref.shape[1:])


def _delta_prep(qkvd, lc, convw, alog, dtb, *, batch):
    t, wd = qkvd.shape
    c = DN_CHUNK
    rows = PREP_CHUNKS * c
    n_steps = t // rows
    steps_per_seq = n_steps // batch
    sub_per_step = rows // SUBLANES
    last_sub = t // SUBLANES - 1
    n_chunks = t // c
    const = lambda j: (0, 0)
    out_specs = [pl.BlockSpec((PREP_CHUNKS * DN_HEADS, 2 * c, DN_HEAD_DIM), lambda j: (j, 0, 0)),
                 pl.BlockSpec((rows, DN_WIDTH), lambda j: (j, 0)),
                 pl.BlockSpec((PREP_CHUNKS * DN_HEAD_DIM, DN_HEADS * c), lambda j: (j, 0)),
                 pl.BlockSpec((rows, DN_HEADS * c), lambda j: (j, 0)),
                 pl.BlockSpec((PREP_CHUNKS * DN_HEADS, SUBLANES, LANES), lambda j: (j, 0, 0))]
    out_shape = [jax.ShapeDtypeStruct((n_chunks * DN_HEADS, 2 * c, DN_HEAD_DIM), BF16),
                 jax.ShapeDtypeStruct((t, DN_WIDTH), F32),
                 jax.ShapeDtypeStruct((n_chunks * DN_HEAD_DIM, DN_HEADS * c), BF16),
                 jax.ShapeDtypeStruct((t, DN_HEADS * c), BF16),
                 jax.ShapeDtypeStruct((n_chunks * DN_HEADS, SUBLANES, LANES), F32)]
    return pl.pallas_call(
        functools.partial(_delta_prep_kernel, steps_per_seq=steps_per_seq),
        grid=(n_steps,),
        in_specs=[pl.BlockSpec((rows, wd), lambda j: (j, 0)),
                  pl.BlockSpec((SUBLANES, wd), lambda j: (jnp.maximum(j * sub_per_step - 1, 0), 0)),
                  pl.BlockSpec((SUBLANES, wd), lambda j: (jnp.minimum((j + 1) * sub_per_step, last_sub), 0)),
                  pl.BlockSpec((rows, lc.shape[1]), lambda j: (j, 0)),
                  pl.BlockSpec(convw.shape, const), pl.BlockSpec(alog.shape, const), pl.BlockSpec(dtb.shape, const)],
        out_specs=out_specs + out_specs,
        out_shape=out_shape + out_shape,
        scratch_shapes=[pltpu.VMEM((rows + 2 * SUBLANES, wd), F32)],
        compiler_params=_params("parallel"),
        name="delta_prep",
    )(qkvd, qkvd, qkvd, lc, convw, alog, dtb)


def _delta_scan_kernel(wq_f, u_f, ktt_f, att_f, gt_f, wq_b, u_b, ktt_b, att_b, gt_b, s0f, s0b,
                       of_ref, ob_ref, sf_ref, sb_ref, state_f, state_b, *, n_steps):
    n = pl.program_id(1)
    c = DN_CHUNK

    @pl.when(n == 0)
    def _():
        state_f[...] = s0f[0]
        state_b[...] = s0b[0]

    mm = functools.partial(jnp.dot, preferred_element_type=F32)
    states = {(d, h): (state_f, state_b)[d][h] for d in range(2) for h in range(DN_HEADS)}
    for ci in range(SCAN_CHUNKS):
        chains = []
        for d, (refs, o_ref, cc) in enumerate((((wq_f, u_f, ktt_f, att_f, gt_f), of_ref, ci),
                                               ((wq_b, u_b, ktt_b, att_b, gt_b), ob_ref, SCAN_CHUNKS - 1 - ci))):
            for h in range(DN_HEADS):
                chains.append(dict(d=d, h=h, refs=refs, o_ref=o_ref, cc=cc, idx=cc * DN_HEADS + h,
                                   rows=slice(cc * c, (cc + 1) * c), hs=slice(h * DN_HEAD_DIM, (h + 1) * DN_HEAD_DIM),
                                   blk=slice(h * c, (h + 1) * c)))
        for ch in chains:
            wq_ref = ch["refs"][0]
            ch["both"] = mm(wq_ref[ch["idx"]], states[ch["d"], ch["h"]].astype(BF16))
        for ch in chains:
            u_ref = ch["refs"][1]
            ch["v_new"] = (u_ref[ch["rows"], ch["hs"]] - ch["both"][:c]).astype(BF16)
        for ch in chains:
            _, _, ktt_ref, att_ref, gt_ref = ch["refs"]
            cc = ch["cc"]
            ch["o_ref"][ch["rows"], ch["hs"]] = ch["both"][c:] + mm(att_ref[ch["rows"], ch["blk"]], ch["v_new"])
            grown = mm(ktt_ref[cc * DN_HEAD_DIM:(cc + 1) * DN_HEAD_DIM, ch["blk"]], ch["v_new"])
            s = states[ch["d"], ch["h"]]
            decayed = s.reshape(DN_HEAD_DIM // SUBLANES, SUBLANES, DN_HEAD_DIM) * gt_ref[ch["idx"]][None]
            states[ch["d"], ch["h"]] = decayed.reshape(DN_HEAD_DIM, DN_HEAD_DIM) + grown
    for d in range(2):
        for h in range(DN_HEADS):
            (state_f, state_b)[d][h] = states[d, h]

    @pl.when(n == n_steps - 1)
    def _():
        sf_ref[0] = state_f[...]
        sb_ref[0] = state_b[...]


def _delta_scan(prep, s0f, s0b, *, batch):
    t = prep[1].shape[0]
    c = DN_CHUNK
    rows = SCAN_CHUNKS * c
    n_steps = t // batch // rows
    fwd = lambda b, n: b * n_steps + n
    bwd = lambda b, n: b * n_steps + n_steps - 1 - n

    def specs(idx):
        return [pl.BlockSpec((SCAN_CHUNKS * DN_HEADS, 2 * c, DN_HEAD_DIM), lambda b, n: (idx(b, n), 0, 0)),
                pl.BlockSpec((rows, DN_WIDTH), lambda b, n: (idx(b, n), 0)),
                pl.BlockSpec((SCAN_CHUNKS * DN_HEAD_DIM, DN_HEADS * c), lambda b, n: (idx(b, n), 0)),
                pl.BlockSpec((rows, DN_HEADS * c), lambda b, n: (idx(b, n), 0)),
                pl.BlockSpec((SCAN_CHUNKS * DN_HEADS, SUBLANES, LANES), lambda b, n: (idx(b, n), 0, 0))]

    state_spec = pl.BlockSpec((1, DN_HEADS, DN_HEAD_DIM, DN_HEAD_DIM), lambda b, n: (b, 0, 0, 0))
    state_shape = jax.ShapeDtypeStruct((batch, DN_HEADS, DN_HEAD_DIM, DN_HEAD_DIM), F32)
    return pl.pallas_call(
        functools.partial(_delta_scan_kernel, n_steps=n_steps),
        grid=(batch, n_steps),
        in_specs=specs(fwd) + specs(bwd) + [state_spec, state_spec],
        out_specs=[pl.BlockSpec((rows, DN_WIDTH), lambda b, n: (fwd(b, n), 0)),
                   pl.BlockSpec((rows, DN_WIDTH), lambda b, n: (bwd(b, n), 0)),
                   state_spec, state_spec],
        out_shape=[jax.ShapeDtypeStruct((t, DN_WIDTH), F32), jax.ShapeDtypeStruct((t, DN_WIDTH), F32),
                   state_shape, state_shape],
        scratch_shapes=[pltpu.VMEM((DN_HEADS, DN_HEAD_DIM, DN_HEAD_DIM), F32),
                        pltpu.VMEM((DN_HEADS, DN_HEAD_DIM, DN_HEAD_DIM), F32)],
        compiler_params=_params("parallel", "arbitrary"),
        name="delta_scan",
    )(*prep, s0f, s0b)


def _delta(qkvd, lc, convw, alog, dtb, s0f, s0b, *, batch):
    return _delta_scan(_delta_prep(qkvd, lc, convw, alog, dtb, batch=batch), s0f, s0b, batch=batch)


def _attn_kernel(q_ref, kp_ref, kc_ref, kn_ref, vp_ref, vc_ref, vn_ref, kx_ref, vx_ref, sink_ref, o_ref, *, n_blocks):
    i = pl.program_id(1)
    qb = ATTN_BLOCK
    span = qb + 2 * ATTN_WINDOW
    n_ctx = kx_ref.shape[0]
    qpos = lax.broadcasted_iota(jnp.int32, (qb, span + n_ctx), 0)
    j = lax.broadcasted_iota(jnp.int32, (qb, span + n_ctx), 1)
    rel = j - qpos
    local = (rel >= 0) & (rel <= 2 * ATTN_WINDOW)
    local = local & ((j >= ATTN_WINDOW) | (i > 0)) & ((j < ATTN_WINDOW + qb) | (i < n_blocks - 1))
    valid = local | (j >= span)

    group = ATTN_HEADS // ATTN_KV_HEADS
    hd = ATTN_HEAD_DIM
    keys, vals = [], []
    for kv in range(ATTN_KV_HEADS):
        sl = slice(kv * hd, (kv + 1) * hd)
        keys.append(jnp.concatenate([kp_ref[:, sl], kc_ref[:, sl], kn_ref[:, sl], kx_ref[:, sl]], axis=0))
        vals.append(jnp.concatenate([vp_ref[:, sl], vc_ref[:, sl], vn_ref[:, sl], vx_ref[:, sl]], axis=0))
    scores = [_dot_nt(q_ref[:, h * hd:(h + 1) * hd], keys[h // group]) for h in range(ATTN_HEADS)]
    probs = []
    for h in range(ATTN_HEADS):
        s = jnp.where(valid, scores[h], -jnp.inf)
        sink = sink_ref[0:1, h:h + 1]
        m = jnp.maximum(jnp.max(s, axis=-1, keepdims=True), sink)
        p = jnp.exp(s - m)
        denom = jnp.sum(p, axis=-1, keepdims=True) + jnp.exp(sink - m)
        probs.append((p / denom).astype(BF16))
    for h in range(ATTN_HEADS):
        o_ref[:, h * hd:(h + 1) * hd] = jnp.dot(probs[h], vals[h // group], preferred_element_type=F32).astype(o_ref.dtype)


def _attention(qa, ka, va, kx, vx, sink, *, batch):
    t = qa.shape[0]
    qb = ATTN_BLOCK
    n_blocks = t // batch // qb
    n_ctx = kx.shape[0] // batch
    cur = lambda b, i: (b * n_blocks + i, 0)
    prev = lambda b, i: (b * n_blocks + jnp.maximum(i - 1, 0), 0)
    nxt = lambda b, i: (b * n_blocks + jnp.minimum(i + 1, n_blocks - 1), 0)
    ctx = lambda b, i: (b, 0)
    kv_specs = [pl.BlockSpec((qb, ATTN_KV), prev), pl.BlockSpec((qb, ATTN_KV), cur), pl.BlockSpec((qb, ATTN_KV), nxt)]
    return pl.pallas_call(
        functools.partial(_attn_kernel, n_blocks=n_blocks),
        grid=(batch, n_blocks),
        in_specs=[pl.BlockSpec((qb, ATTN_Q), cur)] + kv_specs + kv_specs + [
            pl.BlockSpec((n_ctx, ATTN_KV), ctx), pl.BlockSpec((n_ctx, ATTN_KV), ctx),
            pl.BlockSpec(sink.shape, lambda b, i: (0, 0))],
        out_specs=pl.BlockSpec((qb, ATTN_Q), cur),
        out_shape=jax.ShapeDtypeStruct((t, ATTN_Q), BF16),
        compiler_params=_params("parallel", "parallel"),
        name="attention",
    )(qa, ka, ka, ka, va, va, va, kx, vx, sink)


def _merge_kernel(x_ref, gt_ref, of_ref, ob_ref, z_ref, dng_ref, oa_ref, ga_ref, gd_ref, wba_ref, wbd_ref, wo_ref, x1_ref):
    o = of_ref[...] + ob_ref[...]
    z = z_ref[...]
    heads = []
    for h in range(DN_HEADS):
        sl = slice(h * DN_HEAD_DIM, (h + 1) * DN_HEAD_DIM)
        oh = o[:, sl]
        yh = oh * lax.rsqrt(jnp.mean(oh * oh, axis=-1, keepdims=True) + RMS_EPS) * dng_ref[...]
        heads.append((yh * _silu(z[:, sl])).astype(BF16))
    o_dn = jnp.concatenate(heads, axis=1)
    y = (ga_ref[...] * jnp.dot(oa_ref[...], wba_ref[...], preferred_element_type=F32)
         + gd_ref[...] * jnp.dot(o_dn, wbd_ref[...], preferred_element_type=F32))
    x1_ref[...] = x_ref[...] + gt_ref[0] * jnp.dot(y.astype(BF16), wo_ref[...], preferred_element_type=F32)


def _merge(xf, gt1, o_f, o_b, z, dn_gain, o_attn, g_attn, g_dn, wba, wbd, wo, *, tb, blocks_per_batch):
    t, d = xf.shape
    row = lambda i: (i, 0)
    const = lambda i: (0, 0)
    rows = lambda w: pl.BlockSpec((tb, w), row)
    return pl.pallas_call(
        _merge_kernel,
        grid=(t // tb,),
        in_specs=[rows(d), pl.BlockSpec((1, 1, d), lambda i: (i // blocks_per_batch, 0, 0)),
                  rows(DN_WIDTH), rows(DN_WIDTH), rows(DN_WIDTH), pl.BlockSpec(dn_gain.shape, const),
                  rows(ATTN_Q), rows(d), rows(d),
                  pl.BlockSpec(wba.shape, const), pl.BlockSpec(wbd.shape, const), pl.BlockSpec(wo.shape, const)],
        out_specs=rows(d),
        out_shape=jax.ShapeDtypeStruct((t, d), F32),
        compiler_params=_params("parallel"),
        name="merge",
    )(xf, gt1, o_f, o_b, z, dn_gain, o_attn, g_attn, g_dn, wba, wbd, wo)


def _gelu_tanh(x):
    return 0.5 * x * (1.0 + jnp.tanh(math.sqrt(2.0 / math.pi) * (x + 0.044715 * (x * x * x))))


def _top_rows(x, k):
    rows = []
    for _ in range(k):
        m = jnp.max(x, axis=0, keepdims=True)
        rows.append(m)
        x = jnp.where(x == m, -jnp.inf, x)
    return rows


def _peer_kernel(x1_ref, sh_ref, sc_ref, gt_ref, gain_ref, fgain_ref, wqt_ref, keys_ref, u_ref, vt_ref, o_ref,
                 h_scr, s1_scr, a_scr, s2_scr, b_scr, act_scr, wa_scr, acc_scr, *, n_tiles):
    e = pl.program_id(1)
    tile_rows = u_ref.shape[0] // PEER_KEYS
    k = PEER_TOPK

    @pl.when(e == 0)
    def _():
        x1 = x1_ref[...]
        y = x1 * lax.rsqrt(jnp.mean(x1 * x1, axis=-1, keepdims=True) + RMS_EPS) * gain_ref[...]
        ht = (y * (1.0 + sc_ref[0]) + sh_ref[0]).T.astype(BF16)
        h_scr[...] = ht
        qt = jnp.dot(wqt_ref[...], ht, preferred_element_type=F32)
        for h in range(PEER_HEADS):
            halves = []
            for p in range(2):
                r0 = (2 * h + p) * PEER_KEY_HALF
                halves.append(jnp.dot(keys_ref[2 * h + p], qt[r0:r0 + PEER_KEY_HALF].astype(BF16),
                                      preferred_element_type=F32))
            s1, s2 = halves
            top1 = _top_rows(s1, k)
            top2 = _top_rows(s2, k)
            top2_mat = jnp.concatenate(top2, axis=0)
            cand = jnp.concatenate(
                [top1[0] + top2_mat]
                + [top1[r] + top2_mat[:k // 2] for r in range(1, k // 2)]
                + [jnp.concatenate(top1[k // 2:], axis=0) + top2[0]], axis=0)
            best = _top_rows(cand, k)
            zsum = jnp.zeros_like(best[0])
            for r in best:
                zsum = zsum + jnp.exp(r - best[0])
            half_tau = 0.5 * best[-1]
            s1_scr[h] = half_tau - s1
            a_scr[h] = jnp.exp(s1 - top1[0]) / zsum
            s2_scr[h] = (s2 - half_tau).astype(BF16)
            b_scr[h] = jnp.exp(s2 - top2[0]).astype(BF16)
        act_scr[...] = jnp.zeros_like(act_scr)
        wa_scr[...] = jnp.zeros_like(wa_scr)
        acc_scr[...] = jnp.zeros_like(acc_scr)

    slot = e % 2
    prev = 1 - slot
    zero = jnp.zeros((), BF16)

    def accumulate(cols):
        acc_scr[:, cols] += jnp.dot(vt_ref[...], wa_scr[slot, :, cols], preferred_element_type=F32)

    def score(cols):
        act_scr[slot, :, cols] = jnp.dot(u_ref[...], h_scr[:, cols], preferred_element_type=F32)

    def gate(ii, cols):
        i = jnp.clip((e - 1) * tile_rows + ii, 0, PEER_KEYS - 1)
        rows = slice(ii * PEER_KEYS, (ii + 1) * PEER_KEYS)
        xb = act_scr[prev, rows, cols].astype(BF16)
        w = None
        for h in range(PEER_HEADS):
            thr = s1_scr[h, pl.ds(i, 1), cols].astype(BF16)
            term = a_scr[h, pl.ds(i, 1), cols].astype(BF16) * jnp.where(s2_scr[h, :, cols] >= thr, b_scr[h, :, cols], zero)
            w = term if w is None else w + term
        wa_scr[prev, rows, cols] = w * _gelu_tanh(xb)

    n_split = 2
    width = h_scr.shape[1] // n_split
    for n in range(n_split):
        cols = slice(n * width, (n + 1) * width)
        for ii in range(tile_rows // 2):
            gate(ii, cols)
        accumulate(cols)
        for ii in range(tile_rows // 2, tile_rows):
            gate(ii, cols)
        score(cols)

    @pl.when(e == n_tiles + 1)
    def _():
        y = x1_ref[...] + gt_ref[0] * acc_scr[...].T
        o_ref[...] = y * lax.rsqrt(jnp.mean(y * y, axis=-1, keepdims=True) + RMS_EPS) * fgain_ref[...]


def _peer(x1, sh2, sc2, gt2, gain, fgain, wqt, keys, u, vt, *, tb, et, blocks_per_batch):
    t, d = x1.shape
    n_exp = u.shape[0]
    n_tiles = n_exp // et
    row = lambda i, e: (i, 0)
    const2 = lambda i, e: (0, 0)
    mod = lambda i, e: (i // blocks_per_batch, 0, 0)
    kt = (PEER_HEADS, PEER_KEYS, tb)
    return pl.pallas_call(
        functools.partial(_peer_kernel, n_tiles=n_tiles),
        grid=(t // tb, n_tiles + 2),
        in_specs=[pl.BlockSpec((tb, d), row),
                  pl.BlockSpec((1, 1, d), mod), pl.BlockSpec((1, 1, d), mod), pl.BlockSpec((1, 1, d), mod),
                  pl.BlockSpec((1, d), const2), pl.BlockSpec((1, d), const2),
                  pl.BlockSpec(wqt.shape, const2), pl.BlockSpec(keys.shape, lambda i, e: (0, 0, 0)),
                  pl.BlockSpec((et, d), lambda i, e: (jnp.minimum(e, n_tiles - 1), 0)),
                  pl.BlockSpec((d, et), lambda i, e: (0, jnp.clip(e - 2, 0, n_tiles - 1)))],
        out_specs=pl.BlockSpec((tb, d), row),
        out_shape=jax.ShapeDtypeStruct((t, d), F32),
        scratch_shapes=[pltpu.VMEM((d, tb), BF16),
                        pltpu.VMEM(kt, F32), pltpu.VMEM(kt, F32), pltpu.VMEM(kt, BF16), pltpu.VMEM(kt, BF16),
                        pltpu.VMEM((2, et, tb), F32), pltpu.VMEM((2, et, tb), BF16), pltpu.VMEM((d, tb), F32)],
        compiler_params=_params("parallel", "arbitrary"),
        name="peer",
    )(x1, sh2, sc2, gt2, gain, fgain, wqt, keys, u, vt)


def _rope_tables(seq):
    t = jnp.arange(seq, dtype=jnp.int32)
    inv_freq = ROPE_BASE ** (-jnp.arange(ROPE_PAIRS, dtype=F32) / ROPE_PAIRS)
    ang_r = (t // GRID_W).astype(F32)[:, None] * inv_freq[None, :]
    ang_c = (t % GRID_W).astype(F32)[:, None] * inv_freq[None, :]
    cos_h = jnp.concatenate([jnp.cos(ang_r), jnp.cos(ang_r), jnp.cos(ang_c), jnp.cos(ang_c)], axis=1)
    sin_h = jnp.concatenate([-jnp.sin(ang_r), jnp.sin(ang_r), -jnp.sin(ang_c), jnp.sin(ang_c)], axis=1)
    reps = LANES // ATTN_HEAD_DIM
    return jnp.tile(cos_h, (1, reps)), jnp.tile(sin_h, (1, reps))


def _pick_block(n, target):
    b = min(n, target)
    while n % b:
        b //= 2
    return b


def kernel(x, c, ctx, c_ctx, w_ada, b_ada, norm_mix, norm_ffn, w_in, b_gate, attn_sink, dn_conv, dn_a_log_f, dn_dt_bias_f, dn_a_log_b, dn_dt_bias_b, dn_norm, w_br_attn, w_br_dn, w_out, peer_wq, peer_keys, peer_u, peer_v, final_norm):
    bsz, seq, d = x.shape
    n_ctx = ctx.shape[1]
    depth = w_ada.shape[0]
    assert depth == 1, "context-stream update between layers is not implemented"
    assert bsz + 1 <= MOD_ROWS and seq % ATTN_BLOCK == 0 and n_ctx % DN_CHUNK == 0
    layer = 0
    xf = x.reshape(bsz * seq, d)
    ctxf = ctx.reshape(bsz * n_ctx, d)

    cc = jnp.zeros((MOD_ROWS, d), F32).at[:bsz].set(c).at[bsz].set(c_ctx)
    mod = _adaln(cc, w_ada[layer], b_ada[layer])
    sh1, sc1, gt1, sh2, sc2, gt2 = [m.reshape(MOD_ROWS, 1, d) for m in jnp.split(mod, 6, axis=-1)]

    w = w_in[layer].astype(BF16)
    o_qkv = ATTN_Q + 2 * ATTN_KV
    o_d = o_qkv + 3 * DN_WIDTH
    o_zl = o_d + DN_WIDTH + 4 * DN_HEADS
    wa, wd, wzl, wg = w[:, :o_qkv], w[:, o_qkv:o_d], w[:, o_d:o_zl], w[:, o_zl:]
    bg = b_gate[layer].reshape(1, -1)
    gain_mix = norm_mix[layer].reshape(1, d)

    convw = dn_conv[layer]
    alog = jnp.concatenate([dn_a_log_f[layer], dn_a_log_b[layer]]).reshape(1, 2 * DN_HEADS)
    dtb = jnp.concatenate([dn_dt_bias_f[layer], dn_dt_bias_b[layer]]).reshape(1, 2 * DN_HEADS)

    tb_c = _pick_block(bsz * n_ctx, 512)
    _, kx, vx, qkvd_c, _, lc_c, _, _ = _inproj(ctxf, sh1, sc1, lambda i: bsz, gain_mix, wa, wd, wzl, wg, bg, tb=tb_c)
    zero_state = jnp.zeros((bsz, DN_HEADS, DN_HEAD_DIM, DN_HEAD_DIM), F32)
    _, _, st_f, st_b = _delta(qkvd_c, lc_c, convw, alog, dtb, zero_state, zero_state, batch=bsz)

    tb = _pick_block(seq, 512)
    cos, sin = _rope_tables(seq)
    qa, ka, va, qkvd, z, lc, g_attn, g_dn = _inproj(
        xf, sh1, sc1, lambda i: i // (seq // tb), gain_mix, wa, wd, wzl, wg, bg, cos, sin, tb=tb, pos_blocks=seq // tb)
    o_attn = _attention(qa, ka, va, kx, vx, attn_sink[layer].reshape(1, ATTN_HEADS), batch=bsz)
    o_f, o_b, _, _ = _delta(qkvd, lc, convw, alog, dtb, st_f, st_b, batch=bsz)
    x1 = _merge(xf, gt1, o_f, o_b, z, dn_norm[layer].reshape(1, DN_HEAD_DIM), o_attn, g_attn, g_dn,
                w_br_attn[layer].astype(BF16), w_br_dn[layer].astype(BF16), w_out[layer].astype(BF16),
                tb=tb, blocks_per_batch=seq // tb)

    tb_p = _pick_block(seq, 512)
    out = _peer(x1, sh2, sc2, gt2, norm_ffn[layer].reshape(1, d), final_norm.reshape(1, d),
                peer_wq[layer].T.astype(BF16),
                peer_keys[layer].reshape(2 * PEER_HEADS, PEER_KEYS, PEER_KEY_HALF).astype(BF16),
                peer_u[layer].astype(BF16), peer_v[layer].T.astype(BF16),
                tb=tb_p, et=1024, blocks_per_batch=seq // tb_p)
    return out.reshape(bsz, seq, d)
```

```python
import functools
import math

import jax
import jax.numpy as jnp
from jax import lax
from jax.experimental import pallas as pl
from jax.experimental.pallas import tpu as pltpu

F32 = jnp.float32
BF16 = jnp.bfloat16
HIGHEST = lax.Precision.HIGHEST

GRID_W = 64
ATTN_HEADS = 8
ATTN_KV_HEADS = 2
ATTN_HEAD_DIM = 64
ATTN_WINDOW = 128
ATTN_BLOCK = 128
ROPE_BASE = 10000.0
ROPE_PAIRS = ATTN_HEAD_DIM // 4
DN_HEADS = 4
DN_HEAD_DIM = 128
DN_CONV = 5
DN_CHUNK = 64
PEER_HEADS = 8
PEER_KEYS = 128
PEER_KEY_HALF = 64
PEER_TOPK = 16
RMS_EPS = 1e-6
L2_EPS = 1e-6
ATTN_Q = ATTN_HEADS * ATTN_HEAD_DIM
ATTN_KV = ATTN_KV_HEADS * ATTN_HEAD_DIM
DN_WIDTH = DN_HEADS * DN_HEAD_DIM

LANES = 128
SUBLANES = 8
VMEM_LIMIT_BYTES = 56 * 1024 * 1024

MOD_ROWS = 16


def _dot(a, b):
    return jnp.dot(a.astype(BF16), b.astype(BF16), preferred_element_type=F32)


def _dot_nt(a, b):
    return lax.dot_general(a.astype(BF16), b.astype(BF16), (((1,), (1,)), ((), ())), preferred_element_type=F32)


def _dot_tn(a, b):
    return lax.dot_general(a.astype(BF16), b.astype(BF16), (((0,), (0,)), ((), ())), preferred_element_type=F32)


def _dot_hi(a, b):
    return jnp.dot(a, b, precision=HIGHEST, preferred_element_type=F32)


def _silu(x):
    return x * jax.nn.sigmoid(x)


def _softplus(x):
    return jnp.maximum(x, 0.0) + jnp.log1p(jnp.exp(-jnp.abs(x)))


def _params(*sem):
    return pltpu.CompilerParams(dimension_semantics=sem, vmem_limit_bytes=VMEM_LIMIT_BYTES)


def _ada_kernel(c_ref, w_ref, b_ref, o_ref):
    o_ref[...] = _dot_hi(_silu(c_ref[...]), w_ref[...]) + b_ref[...]


def _adaln(cc, w_ada, b_ada):
    d = cc.shape[1]
    n = w_ada.shape[1]
    tn = 512
    return pl.pallas_call(
        _ada_kernel,
        grid=(n // tn,),
        in_specs=[pl.BlockSpec((MOD_ROWS, d), lambda j: (0, 0)),
                  pl.BlockSpec((d, tn), lambda j: (0, j)),
                  pl.BlockSpec((1, tn), lambda j: (0, j))],
        out_specs=pl.BlockSpec((MOD_ROWS, tn), lambda j: (0, j)),
        out_shape=jax.ShapeDtypeStruct((MOD_ROWS, n), F32),
        compiler_params=_params("parallel"),
        name="adaln",
    )(cc, w_ada, b_ada.reshape(1, n))


def _rope(x, cos, sin):
    lane = lax.broadcasted_iota(jnp.int32, x.shape, 1)
    partner = jnp.where((lane % 32) < 16, pltpu.roll(x, LANES - 16, 1), pltpu.roll(x, 16, 1))
    return x * cos + partner * sin


def _inproj_kernel(*refs, rope):
    if rope:
        (x_ref, sh_ref, sc_ref, gain_ref, wa_ref, wd_ref, wzl_ref, wg_ref, bg_ref, cos_ref, sin_ref,
         qa_ref, ka_ref, va_ref, qkvd_ref, z_ref, lc_ref, ga_ref, gd_ref) = refs
    else:
        (x_ref, sh_ref, sc_ref, gain_ref, wa_ref, wd_ref, wzl_ref, wg_ref, bg_ref,
         qa_ref, ka_ref, va_ref, qkvd_ref, z_ref, lc_ref, ga_ref, gd_ref) = refs
    x = x_ref[...]
    y = x * lax.rsqrt(jnp.mean(x * x, axis=-1, keepdims=True) + RMS_EPS) * gain_ref[...]
    hb = (y * (1.0 + sc_ref[0]) + sh_ref[0]).astype(BF16)

    pa = jnp.dot(hb, wa_ref[...], preferred_element_type=F32)
    q = pa[:, :ATTN_Q] * (ATTN_HEAD_DIM ** -0.5)
    k = pa[:, ATTN_Q:ATTN_Q + ATTN_KV]
    if rope:
        cos = cos_ref[...]
        sin = sin_ref[...]
        for j in range(ATTN_Q // LANES):
            qa_ref[:, j * LANES:(j + 1) * LANES] = _rope(q[:, j * LANES:(j + 1) * LANES], cos, sin).astype(BF16)
        ka_ref[...] = _rope(k, cos, sin).astype(BF16)
    else:
        qa_ref[...] = q.astype(BF16)
        ka_ref[...] = k.astype(BF16)
    va_ref[...] = pa[:, ATTN_Q + ATTN_KV:].astype(BF16)

    qkvd_ref[...] = jnp.dot(hb, wd_ref[...], preferred_element_type=F32)
    pzl = jnp.dot(hb, wzl_ref[...], preferred_element_type=F32)
    z_ref[...] = pzl[:, :DN_WIDTH].astype(z_ref.dtype)
    lc_ref[...] = pzl[:, DN_WIDTH:]
    gates = jax.nn.sigmoid(jnp.dot(hb, wg_ref[...], preferred_element_type=F32) + bg_ref[...])
    d = ga_ref.shape[1]
    ga_ref[...] = gates[:, :d].astype(ga_ref.dtype)
    gd_ref[...] = gates[:, d:].astype(gd_ref.dtype)


def _inproj(xf, sh, sc, mod_index, gain, wa, wd, wzl, wg, bg, cos=None, sin=None, *, tb, pos_blocks=1):
    t, d = xf.shape
    rope = cos is not None
    row = lambda i: (i, 0)
    const = lambda i: (0, 0)
    mod = lambda i: (mod_index(i), 0, 0)
    in_specs = [pl.BlockSpec((tb, d), row),
                pl.BlockSpec((1, 1, d), mod), pl.BlockSpec((1, 1, d), mod),
                pl.BlockSpec((1, d), const),
                pl.BlockSpec(wa.shape, const), pl.BlockSpec(wd.shape, const),
                pl.BlockSpec(wzl.shape, const), pl.BlockSpec(wg.shape, const), pl.BlockSpec(bg.shape, const)]
    args = [xf, sh, sc, gain, wa, wd, wzl, wg, bg]
    if rope:
        pos = lambda i: (i % pos_blocks, 0)
        in_specs += [pl.BlockSpec((tb, LANES), pos), pl.BlockSpec((tb, LANES), pos)]
        args += [cos, sin]
    widths = [(ATTN_Q, BF16), (ATTN_KV, BF16), (ATTN_KV, BF16), (3 * DN_WIDTH, F32), (DN_WIDTH, BF16),
              (4 * DN_HEADS, F32), (d, BF16), (d, BF16)]
    return pl.pallas_call(
        functools.partial(_inproj_kernel, rope=rope),
        grid=(t // tb,),
        in_specs=in_specs,
        out_specs=[pl.BlockSpec((tb, w), row) for w, _ in widths],
        out_shape=[jax.ShapeDtypeStruct((t, w), dt) for w, dt in widths],
        compiler_params=_params("parallel"),
        name="inproj_rope" if rope else "inproj",
    )(*args)


def _l2norm(x):
    return x * lax.rsqrt(jnp.sum(x * x, axis=-1, keepdims=True) + L2_EPS)


def _split_bf16(x):
    hi = x.astype(BF16).astype(F32)
    return hi, x - hi


def _dot_3pass(a, b):
    ah, al = _split_bf16(a)
    bh, bl = _split_bf16(b)
    return _dot(jnp.concatenate([ah, ah, al], axis=1), jnp.concatenate([bh, bl, bh], axis=0))


PREP_CHUNKS = 4
SCAN_CHUNKS = 4


def _delta_prep_kernel(cur_ref, prev_ref, next_ref, lc_ref, convw_ref, alog_ref, dtb_ref,
                       wq_f, u_f, ktt_f, att_f, gt_f, wq_b, u_b, ktt_b, att_b, gt_b, ext_ref, *, steps_per_seq):
    j = pl.program_id(0)
    c = DN_CHUNK
    rows = PREP_CHUNKS * c
    halo = SUBLANES - DN_CONV // 2
    first = (j % steps_per_seq) == 0
    last_step = (j % steps_per_seq) == steps_per_seq - 1
    ext_ref[0:SUBLANES, :] = prev_ref[...] * (1.0 - first.astype(F32))
    ext_ref[SUBLANES:SUBLANES + rows, :] = cur_ref[...]
    ext_ref[SUBLANES + rows:, :] = next_ref[...] * (1.0 - last_step.astype(F32))
    acc = ext_ref[pl.ds(halo, rows), :] * convw_ref[0:1, :]
    for tap in range(1, DN_CONV):
        acc = acc + ext_ref[pl.ds(halo + tap, rows), :] * convw_ref[tap:tap + 1, :]
    u = _silu(acc)

    lc = lc_ref[...]
    nd = 2 * DN_HEADS
    g_all = -jnp.exp(alog_ref[...]) * _softplus(lc[:, :nd] + dtb_ref[...])
    beta_all = jax.nn.sigmoid(lc[:, nd:])

    pair = 2 * c
    row = lax.broadcasted_iota(jnp.int32, (pair, pair), 0)
    col = lax.broadcasted_iota(jnp.int32, (pair, pair), 1)
    same_head = (row // c) == (col // c)
    eye = (row == col).astype(F32)
    row_c = lax.broadcasted_iota(jnp.int32, (c, c), 0)
    col_c = lax.broadcasted_iota(jnp.int32, (c, c), 1)
    lower_f = (row_c >= col_c).astype(F32)
    upper_f = (row_c <= col_c).astype(F32)
    is_fwd_col = lax.broadcasted_iota(jnp.int32, (c, nd), 1) < DN_HEADS
    first_head_lanes = lax.broadcasted_iota(jnp.int32, (c, pair), 1) < c

    out_refs = ((wq_f, u_f, ktt_f, att_f, gt_f), (wq_b, u_b, ktt_b, att_b, gt_b))
    chains = []
    for ci in range(PREP_CHUNKS):
        r0 = ci * c
        g = g_all[r0:r0 + c]
        cum = jnp.where(is_fwd_col, _dot_hi(lower_f, g), _dot_hi(upper_f, g))
        cum_t = cum.T
        for h0 in range(0, DN_HEADS, 2):
            def stacked(off):
                return jnp.concatenate([u[r0:r0 + c, off + h * DN_HEAD_DIM:off + (h + 1) * DN_HEAD_DIM]
                                        for h in (h0, h0 + 1)], axis=0)
            q = _l2norm(stacked(0)) * (DN_HEAD_DIM ** -0.5)
            k = _l2norm(stacked(DN_WIDTH))
            v = stacked(2 * DN_WIDTH)
            for d in range(2):
                reverse = d == 1
                cc = d * DN_HEADS + h0
                last = 0 if reverse else c - 1
                gcol = jnp.concatenate([cum[:, cc:cc + 1], cum[:, cc + 1:cc + 2]], axis=0)
                grow = jnp.concatenate([cum_t[cc:cc + 1, :], cum_t[cc + 1:cc + 2, :]], axis=1)
                glast_h = [cum[last:last + 1, cc + i:cc + i + 1] for i in range(2)]
                incl = same_head & ((row <= col) if reverse else (row >= col))
                chains.append(dict(
                    ci=ci, d=d, h0=h0, q=q, k=k, v=v, gcol=gcol, glast_h=glast_h,
                    glast=jnp.concatenate([jnp.broadcast_to(gl, (c, 1)) for gl in glast_h], axis=0),
                    bcol=jnp.concatenate([beta_all[r0:r0 + c, cc:cc + 1], beta_all[r0:r0 + c, cc + 1:cc + 2]], axis=0),
                    decay=jnp.exp(jnp.where(incl, gcol - grow, -jnp.inf)),
                    strict=same_head & ((row < col) if reverse else (row > col))))

    for ch in chains:
        ch["kb"] = ch["k"] * ch["bcol"]
        scores = _dot_nt(jnp.concatenate([ch["kb"], ch["q"]], axis=0), ch["k"])
        ch["x"] = -jnp.where(ch["strict"], scores[:pair] * ch["decay"], 0.0)
        ch["attn"] = scores[pair:] * ch["decay"]
    for ch in chains:
        ch["t"] = eye + ch["x"]
        ch["qpow"] = _dot_3pass(ch["x"], ch["x"])
    levels = int(math.log2(c)) - 1
    for lvl in range(levels):
        for ch in chains:
            if lvl < levels - 1:
                both = _dot_3pass(ch["qpow"], jnp.concatenate([ch["t"], ch["qpow"]], axis=1))
                ch["t"] = ch["t"] + both[:, :pair]
                ch["qpow"] = both[:, pair:]
            else:
                ch["t"] = ch["t"] + _dot_3pass(ch["qpow"], ch["t"])
    for ch in chains:
        eg = jnp.exp(ch["gcol"])
        ch["wu"] = _dot(ch["t"], jnp.concatenate([ch["kb"] * eg, ch["v"] * ch["bcol"]], axis=1))
        ch["qd"] = ch["q"] * eg
    for ch in chains:
        wq_ref, u_ref, ktt_ref, att_ref, gt_ref = out_refs[ch["d"]]
        ci, h0 = ch["ci"], ch["h0"]
        for i in range(2):
            head_rows = slice(i * c, (i + 1) * c)
            idx = ci * DN_HEADS + h0 + i
            wq_ref[idx, 0:c, :] = ch["wu"][head_rows, :DN_HEAD_DIM].astype(BF16)
            wq_ref[idx, c:, :] = ch["qd"][head_rows].astype(BF16)
            u_ref[ci * c:(ci + 1) * c, (h0 + i) * DN_HEAD_DIM:(h0 + i + 1) * DN_HEAD_DIM] = ch["wu"][head_rows, DN_HEAD_DIM:]
            gt_ref[idx] = jnp.broadcast_to(jnp.exp(ch["glast_h"][i]), gt_ref.shape[1:])
        k_tail = ch["k"] * jnp.exp(ch["glast"] - ch["gcol"])
        ktt_ref[ci * DN_HEAD_DIM:(ci + 1) * DN_HEAD_DIM, h0 * c:(h0 + 2) * c] = k_tail.T.astype(BF16)
        att_ref[ci * c:(ci + 1) * c, h0 * c:(h0 + 2) * c] = jnp.where(
            first_head_lanes, ch["attn"][:c], ch["attn"][c:]).astype(BF16)


def _delta_prep(qkvd, lc, convw, alog, dtb, *, batch):
    t, wd = qkvd.shape
    c = DN_CHUNK
    rows = PREP_CHUNKS * c
    n_steps = t // rows
    steps_per_seq = n_steps // batch
    sub_per_step = rows // SUBLANES
    last_sub = t // SUBLANES - 1
    n_chunks = t // c
    const = lambda j: (0, 0)
    out_specs = [pl.BlockSpec((PREP_CHUNKS * DN_HEADS, 2 * c, DN_HEAD_DIM), lambda j: (j, 0, 0)),
                 pl.BlockSpec((rows, DN_WIDTH), lambda j: (j, 0)),
                 pl.BlockSpec((PREP_CHUNKS * DN_HEAD_DIM, DN_HEADS * c), lambda j: (j, 0)),
                 pl.BlockSpec((rows, DN_HEADS * c), lambda j: (j, 0)),
                 pl.BlockSpec((PREP_CHUNKS * DN_HEADS, SUBLANES, LANES), lambda j: (j, 0, 0))]
    out_shape = [jax.ShapeDtypeStruct((n_chunks * DN_HEADS, 2 * c, DN_HEAD_DIM), BF16),
                 jax.ShapeDtypeStruct((t, DN_WIDTH), F32),
                 jax.ShapeDtypeStruct((n_chunks * DN_HEAD_DIM, DN_HEADS * c), BF16),
                 jax.ShapeDtypeStruct((t, DN_HEADS * c), BF16),
                 jax.ShapeDtypeStruct((n_chunks * DN_HEADS, SUBLANES, LANES), F32)]
    return pl.pallas_call(
        functools.partial(_delta_prep_kernel, steps_per_seq=steps_per_seq),
        grid=(n_steps,),
        in_specs=[pl.BlockSpec((rows, wd), lambda j: (j, 0)),
                  pl.BlockSpec((SUBLANES, wd), lambda j: (jnp.maximum(j * sub_per_step - 1, 0), 0)),
                  pl.BlockSpec((SUBLANES, wd), lambda j: (jnp.minimum((j + 1) * sub_per_step, last_sub), 0)),
                  pl.BlockSpec((rows, lc.shape[1]), lambda j: (j, 0)),
                  pl.BlockSpec(convw.shape, const), pl.BlockSpec(alog.shape, const), pl.BlockSpec(dtb.shape, const)],
        out_specs=out_specs + out_specs,
        out_shape=out_shape + out_shape,
        scratch_shapes=[pltpu.VMEM((rows + 2 * SUBLANES, wd), F32)],
        compiler_params=_params("parallel"),
        name="delta_prep",
    )(qkvd, qkvd, qkvd, lc, convw, alog, dtb)


def _delta_scan_kernel(wq_f, u_f, ktt_f, att_f, gt_f, wq_b, u_b, ktt_b, att_b, gt_b, s0f, s0b,
                       of_ref, ob_ref, sf_ref, sb_ref, state_f, state_b, *, n_steps):
    n = pl.program_id(1)
    c = DN_CHUNK

    @pl.when(n == 0)
    def _():
        state_f[...] = s0f[0]
        state_b[...] = s0b[0]

    mm = functools.partial(jnp.dot, preferred_element_type=F32)
    states = {(d, h): (state_f, state_b)[d][h] for d in range(2) for h in range(DN_HEADS)}
    for ci in range(SCAN_CHUNKS):
        chains = []
        for d, (refs, o_ref, cc) in enumerate((((wq_f, u_f, ktt_f, att_f, gt_f), of_ref, ci),
                                               ((wq_b, u_b, ktt_b, att_b, gt_b), ob_ref, SCAN_CHUNKS - 1 - ci))):
            for h in range(DN_HEADS):
                chains.append(dict(d=d, h=h, refs=refs, o_ref=o_ref, cc=cc, idx=cc * DN_HEADS + h,
                                   rows=slice(cc * c, (cc + 1) * c), hs=slice(h * DN_HEAD_DIM, (h + 1) * DN_HEAD_DIM),
                                   blk=slice(h * c, (h + 1) * c)))
        for ch in chains:
            wq_ref = ch["refs"][0]
            ch["both"] = mm(wq_ref[ch["idx"]], states[ch["d"], ch["h"]].astype(BF16))
        for ch in chains:
            u_ref = ch["refs"][1]
            ch["v_new"] = (u_ref[ch["rows"], ch["hs"]] - ch["both"][:c]).astype(BF16)
        for ch in chains:
            _, _, ktt_ref, att_ref, gt_ref = ch["refs"]
            cc = ch["cc"]
            o = ch["both"][c:] + mm(att_ref[ch["rows"], ch["blk"]], ch["v_new"])
            ch["o_ref"][ch["rows"], ch["hs"]] = o.astype(ch["o_ref"].dtype)
            grown = mm(ktt_ref[cc * DN_HEAD_DIM:(cc + 1) * DN_HEAD_DIM, ch["blk"]], ch["v_new"])
            s = states[ch["d"], ch["h"]]
            decayed = s.reshape(DN_HEAD_DIM // SUBLANES, SUBLANES, DN_HEAD_DIM) * gt_ref[ch["idx"]][None]
            states[ch["d"], ch["h"]] = decayed.reshape(DN_HEAD_DIM, DN_HEAD_DIM) + grown
    for d in range(2):
        for h in range(DN_HEADS):
            (state_f, state_b)[d][h] = states[d, h]

    @pl.when(n == n_steps - 1)
    def _():
        sf_ref[0] = state_f[...]
        sb_ref[0] = state_b[...]


def _delta_scan(prep, s0f, s0b, *, batch):
    t = prep[1].shape[0]
    c = DN_CHUNK
    rows = SCAN_CHUNKS * c
    n_steps = t // batch // rows
    fwd = lambda b, n: b * n_steps + n
    bwd = lambda b, n: b * n_steps + n_steps - 1 - n

    def specs(idx):
        return [pl.BlockSpec((SCAN_CHUNKS * DN_HEADS, 2 * c, DN_HEAD_DIM), lambda b, n: (idx(b, n), 0, 0)),
                pl.BlockSpec((rows, DN_WIDTH), lambda b, n: (idx(b, n), 0)),
                pl.BlockSpec((SCAN_CHUNKS * DN_HEAD_DIM, DN_HEADS * c), lambda b, n: (idx(b, n), 0)),
                pl.BlockSpec((rows, DN_HEADS * c), lambda b, n: (idx(b, n), 0)),
                pl.BlockSpec((SCAN_CHUNKS * DN_HEADS, SUBLANES, LANES), lambda b, n: (idx(b, n), 0, 0))]

    state_spec = pl.BlockSpec((1, DN_HEADS, DN_HEAD_DIM, DN_HEAD_DIM), lambda b, n: (b, 0, 0, 0))
    state_shape = jax.ShapeDtypeStruct((batch, DN_HEADS, DN_HEAD_DIM, DN_HEAD_DIM), F32)
    return pl.pallas_call(
        functools.partial(_delta_scan_kernel, n_steps=n_steps),
        grid=(batch, n_steps),
        in_specs=specs(fwd) + specs(bwd) + [state_spec, state_spec],
        out_specs=[pl.BlockSpec((rows, DN_WIDTH), lambda b, n: (fwd(b, n), 0)),
                   pl.BlockSpec((rows, DN_WIDTH), lambda b, n: (bwd(b, n), 0)),
                   state_spec, state_spec],
        out_shape=[jax.ShapeDtypeStruct((t, DN_WIDTH), BF16), jax.ShapeDtypeStruct((t, DN_WIDTH), BF16),
                   state_shape, state_shape],
        scratch_shapes=[pltpu.VMEM((DN_HEADS, DN_HEAD_DIM, DN_HEAD_DIM), F32),
                        pltpu.VMEM((DN_HEADS, DN_HEAD_DIM, DN_HEAD_DIM), F32)],
        compiler_params=_params("parallel", "arbitrary"),
        name="delta_scan",
    )(*prep, s0f, s0b)


def _delta(qkvd, lc, convw, alog, dtb, s0f, s0b, *, batch):
    return _delta_scan(_delta_prep(qkvd, lc, convw, alog, dtb, batch=batch), s0f, s0b, batch=batch)


def _attn_kernel(q_ref, kp_ref, kc_ref, kn_ref, vp_ref, vc_ref, vn_ref, kx_ref, vx_ref, sink_ref, o_ref, *, n_blocks):
    i = pl.program_id(1)
    qb = ATTN_BLOCK
    span = qb + 2 * ATTN_WINDOW
    n_ctx = kx_ref.shape[0]
    qpos = lax.broadcasted_iota(jnp.int32, (qb, span + n_ctx), 0)
    j = lax.broadcasted_iota(jnp.int32, (qb, span + n_ctx), 1)
    rel = j - qpos
    local = (rel >= 0) & (rel <= 2 * ATTN_WINDOW)
    local = local & ((j >= ATTN_WINDOW) | (i > 0)) & ((j < ATTN_WINDOW + qb) | (i < n_blocks - 1))
    valid = local | (j >= span)

    group = ATTN_HEADS // ATTN_KV_HEADS
    hd = ATTN_HEAD_DIM
    keys, vals = [], []
    for kv in range(ATTN_KV_HEADS):
        sl = slice(kv * hd, (kv + 1) * hd)
        keys.append(jnp.concatenate([kp_ref[:, sl], kc_ref[:, sl], kn_ref[:, sl], kx_ref[:, sl]], axis=0))
        vals.append(jnp.concatenate([vp_ref[:, sl], vc_ref[:, sl], vn_ref[:, sl], vx_ref[:, sl]], axis=0))
    scores = [_dot_nt(q_ref[:, h * hd:(h + 1) * hd], keys[h // group]) for h in range(ATTN_HEADS)]
    probs = []
    for h in range(ATTN_HEADS):
        s = jnp.where(valid, scores[h], -jnp.inf)
        sink = sink_ref[0:1, h:h + 1]
        m = jnp.maximum(jnp.max(s, axis=-1, keepdims=True), sink)
        p = jnp.exp(s - m)
        denom = jnp.sum(p, axis=-1, keepdims=True) + jnp.exp(sink - m)
        probs.append((p / denom).astype(BF16))
    for h in range(ATTN_HEADS):
        o_ref[:, h * hd:(h + 1) * hd] = jnp.dot(probs[h], vals[h // group], preferred_element_type=F32).astype(o_ref.dtype)


def _attention(qa, ka, va, kx, vx, sink, *, batch):
    t = qa.shape[0]
    qb = ATTN_BLOCK
    n_blocks = t // batch // qb
    n_ctx = kx.shape[0] // batch
    cur = lambda b, i: (b * n_blocks + i, 0)
    prev = lambda b, i: (b * n_blocks + jnp.maximum(i - 1, 0), 0)
    nxt = lambda b, i: (b * n_blocks + jnp.minimum(i + 1, n_blocks - 1), 0)
    ctx = lambda b, i: (b, 0)
    kv_specs = [pl.BlockSpec((qb, ATTN_KV), prev), pl.BlockSpec((qb, ATTN_KV), cur), pl.BlockSpec((qb, ATTN_KV), nxt)]
    return pl.pallas_call(
        functools.partial(_attn_kernel, n_blocks=n_blocks),
        grid=(batch, n_blocks),
        in_specs=[pl.BlockSpec((qb, ATTN_Q), cur)] + kv_specs + kv_specs + [
            pl.BlockSpec((n_ctx, ATTN_KV), ctx), pl.BlockSpec((n_ctx, ATTN_KV), ctx),
            pl.BlockSpec(sink.shape, lambda b, i: (0, 0))],
        out_specs=pl.BlockSpec((qb, ATTN_Q), cur),
        out_shape=jax.ShapeDtypeStruct((t, ATTN_Q), BF16),
        compiler_params=_params("parallel", "parallel"),
        name="attention",
    )(qa, ka, ka, ka, va, va, va, kx, vx, sink)


def _merge_kernel(x_ref, gt_ref, of_ref, ob_ref, z_ref, dng_ref, oa_ref, ga_ref, gd_ref, wba_ref, wbd_ref, wo_ref, x1_ref):
    o = of_ref[...].astype(F32) + ob_ref[...].astype(F32)
    z = z_ref[...].astype(F32)
    heads = []
    for h in range(DN_HEADS):
        sl = slice(h * DN_HEAD_DIM, (h + 1) * DN_HEAD_DIM)
        oh = o[:, sl]
        yh = oh * lax.rsqrt(jnp.mean(oh * oh, axis=-1, keepdims=True) + RMS_EPS) * dng_ref[...]
        heads.append((yh * _silu(z[:, sl])).astype(BF16))
    o_dn = jnp.concatenate(heads, axis=1)
    y = (ga_ref[...] * jnp.dot(oa_ref[...], wba_ref[...], preferred_element_type=F32)
         + gd_ref[...] * jnp.dot(o_dn, wbd_ref[...], preferred_element_type=F32))
    x1_ref[...] = x_ref[...] + gt_ref[0] * jnp.dot(y.astype(BF16), wo_ref[...], preferred_element_type=F32)


def _merge(xf, gt1, o_f, o_b, z, dn_gain, o_attn, g_attn, g_dn, wba, wbd, wo, *, tb, blocks_per_batch):
    t, d = xf.shape
    row = lambda i: (i, 0)
    const = lambda i: (0, 0)
    rows = lambda w: pl.BlockSpec((tb, w), row)
    return pl.pallas_call(
        _merge_kernel,
        grid=(t // tb,),
        in_specs=[rows(d), pl.BlockSpec((1, 1, d), lambda i: (i // blocks_per_batch, 0, 0)),
                  rows(DN_WIDTH), rows(DN_WIDTH), rows(DN_WIDTH), pl.BlockSpec(dn_gain.shape, const),
                  rows(ATTN_Q), rows(d), rows(d),
                  pl.BlockSpec(wba.shape, const), pl.BlockSpec(wbd.shape, const), pl.BlockSpec(wo.shape, const)],
        out_specs=rows(d),
        out_shape=jax.ShapeDtypeStruct((t, d), F32),
        compiler_params=_params("parallel"),
        name="merge",
    )(xf, gt1, o_f, o_b, z, dn_gain, o_attn, g_attn, g_dn, wba, wbd, wo)


GATE_DTYPE = BF16


def _gelu_tanh(x):
    return 0.5 * x * (1.0 + jnp.tanh(math.sqrt(2.0 / math.pi) * (x + 0.044715 * (x * x * x))))


def _top_rows(x, k):
    rows = []
    for _ in range(k):
        m = jnp.max(x, axis=0, keepdims=True)
        rows.append(m)
        x = jnp.where(x == m, -jnp.inf, x)
    return rows


def _peer_kernel(x1_ref, sh_ref, sc_ref, gt_ref, gain_ref, fgain_ref, wqt_ref, keys_ref, u_ref, vt_ref, o_ref,
                 h_scr, thr_scr, a_scr, b_scr, act_scr, wa_scr, acc_scr, *, n_tiles):
    e = pl.program_id(1)
    tile_rows = u_ref.shape[0] // PEER_KEYS
    k = PEER_TOPK
    tb = x1_ref.shape[0]

    @pl.when(e == 0)
    def _():
        x1 = x1_ref[...]
        y = x1 * lax.rsqrt(jnp.mean(x1 * x1, axis=-1, keepdims=True) + RMS_EPS) * gain_ref[...]
        ht = (y * (1.0 + sc_ref[0]) + sh_ref[0]).T.astype(BF16)
        h_scr[...] = ht
        qt = jnp.dot(wqt_ref[...], ht, preferred_element_type=F32)
        for h in range(PEER_HEADS):
            halves = []
            for p in range(2):
                r0 = (2 * h + p) * PEER_KEY_HALF
                halves.append(jnp.dot(keys_ref[2 * h + p], qt[r0:r0 + PEER_KEY_HALF].astype(BF16),
                                      preferred_element_type=F32))
            s1, s2 = halves
            top1 = _top_rows(s1, k)
            top2 = _top_rows(s2, k)
            top2_mat = jnp.concatenate(top2, axis=0)
            cand = jnp.concatenate(
                [top1[0] + top2_mat]
                + [top1[r] + top2_mat[:k // 2] for r in range(1, k // 2)]
                + [jnp.concatenate(top1[k // 2:], axis=0) + top2[0]], axis=0)
            best = _top_rows(cand, k)
            zsum = jnp.zeros_like(best[0])
            for r in best:
                zsum = zsum + jnp.exp(r - best[0])
            thr_scr[h] = jnp.exp(best[-1] - s1 - top2[0])
            a_scr[h] = jnp.exp(s1 - top1[0]) / zsum
            b_scr[h] = jnp.exp(s2 - top2[0]).astype(GATE_DTYPE)
        acc_scr[...] = jnp.zeros_like(acc_scr)

    zero = jnp.zeros((), GATE_DTYPE)

    def gate(ii, cols):
        i = e * tile_rows + ii
        rows = slice(ii * PEER_KEYS, (ii + 1) * PEER_KEYS)
        w = None
        for h in range(PEER_HEADS):
            thr = thr_scr[h, pl.ds(i, 1), cols].astype(GATE_DTYPE)
            b = b_scr[h, :, cols]
            term = a_scr[h, pl.ds(i, 1), cols].astype(GATE_DTYPE) * jnp.where(b >= thr, b, zero)
            w = term if w is None else w + term
        wa_scr[rows, cols] = (w * _gelu_tanh(act_scr[rows, cols].astype(GATE_DTYPE))).astype(BF16)

    n_split = 2
    width = tb // n_split
    halves = [slice(n * width, (n + 1) * width) for n in range(n_split)]
    for cols in halves:
        act_scr[:, cols] = jnp.dot(u_ref[...], h_scr[:, cols], preferred_element_type=F32)
    for cols in halves:
        for ii in range(tile_rows):
            gate(ii, cols)
        acc_scr[:, cols] += jnp.dot(vt_ref[...], wa_scr[:, cols], preferred_element_type=F32)

    @pl.when(e == n_tiles - 1)
    def _():
        y = x1_ref[...] + gt_ref[0] * acc_scr[...].T
        o_ref[...] = y * lax.rsqrt(jnp.mean(y * y, axis=-1, keepdims=True) + RMS_EPS) * fgain_ref[...]


def _peer(x1, sh2, sc2, gt2, gain, fgain, wqt, keys, u, vt, *, tb, et, blocks_per_batch):
    t, d = x1.shape
    n_exp = u.shape[0]
    n_tiles = n_exp // et
    row = lambda i, e: (i, 0)
    const2 = lambda i, e: (0, 0)
    mod = lambda i, e: (i // blocks_per_batch, 0, 0)
    kt = (PEER_HEADS, PEER_KEYS, tb)
    return pl.pallas_call(
        functools.partial(_peer_kernel, n_tiles=n_tiles),
        grid=(t // tb, n_tiles),
        in_specs=[pl.BlockSpec((tb, d), row),
                  pl.BlockSpec((1, 1, d), mod), pl.BlockSpec((1, 1, d), mod), pl.BlockSpec((1, 1, d), mod),
                  pl.BlockSpec((1, d), const2), pl.BlockSpec((1, d), const2),
                  pl.BlockSpec(wqt.shape, const2), pl.BlockSpec(keys.shape, lambda i, e: (0, 0, 0)),
                  pl.BlockSpec((et, d), lambda i, e: (e, 0)),
                  pl.BlockSpec((d, et), lambda i, e: (0, e))],
        out_specs=pl.BlockSpec((tb, d), row),
        out_shape=jax.ShapeDtypeStruct((t, d), F32),
        scratch_shapes=[pltpu.VMEM((d, tb), BF16),
                        pltpu.VMEM(kt, F32), pltpu.VMEM(kt, F32), pltpu.VMEM(kt, GATE_DTYPE),
                        pltpu.VMEM((et, tb), F32), pltpu.VMEM((et, tb), BF16), pltpu.VMEM((d, tb), F32)],
        compiler_params=_params("parallel", "arbitrary"),
        name="peer",
    )(x1, sh2, sc2, gt2, gain, fgain, wqt, keys, u, vt)


def _rope_tables(seq):
    t = jnp.arange(seq, dtype=jnp.int32)
    inv_freq = ROPE_BASE ** (-jnp.arange(ROPE_PAIRS, dtype=F32) / ROPE_PAIRS)
    ang_r = (t // GRID_W).astype(F32)[:, None] * inv_freq[None, :]
    ang_c = (t % GRID_W).astype(F32)[:, None] * inv_freq[None, :]
    cos_h = jnp.concatenate([jnp.cos(ang_r), jnp.cos(ang_r), jnp.cos(ang_c), jnp.cos(ang_c)], axis=1)
    sin_h = jnp.concatenate([-jnp.sin(ang_r), jnp.sin(ang_r), -jnp.sin(ang_c), jnp.sin(ang_c)], axis=1)
    reps = LANES // ATTN_HEAD_DIM
    return jnp.tile(cos_h, (1, reps)), jnp.tile(sin_h, (1, reps))


def _pick_block(n, target):
    b = min(n, target)
    while n % b:
        b //= 2
    return b


def kernel(x, c, ctx, c_ctx, w_ada, b_ada, norm_mix, norm_ffn, w_in, b_gate, attn_sink, dn_conv, dn_a_log_f, dn_dt_bias_f, dn_a_log_b, dn_dt_bias_b, dn_norm, w_br_attn, w_br_dn, w_out, peer_wq, peer_keys, peer_u, peer_v, final_norm):
    bsz, seq, d = x.shape
    n_ctx = ctx.shape[1]
    depth = w_ada.shape[0]
    assert depth == 1, "context-stream update between layers is not implemented"
    assert bsz + 1 <= MOD_ROWS and seq % ATTN_BLOCK == 0 and n_ctx % DN_CHUNK == 0
    layer = 0
    xf = x.reshape(bsz * seq, d)
    ctxf = ctx.reshape(bsz * n_ctx, d)

    cc = jnp.zeros((MOD_ROWS, d), F32).at[:bsz].set(c).at[bsz].set(c_ctx)
    mod = _adaln(cc, w_ada[layer], b_ada[layer])
    sh1, sc1, gt1, sh2, sc2, gt2 = [m.reshape(MOD_ROWS, 1, d) for m in jnp.split(mod, 6, axis=-1)]

    w = w_in[layer].astype(BF16)
    o_qkv = ATTN_Q + 2 * ATTN_KV
    o_d = o_qkv + 3 * DN_WIDTH
    o_zl = o_d + DN_WIDTH + 4 * DN_HEADS
    wa, wd, wzl, wg = w[:, :o_qkv], w[:, o_qkv:o_d], w[:, o_d:o_zl], w[:, o_zl:]
    bg = b_gate[layer].reshape(1, -1)
    gain_mix = norm_mix[layer].reshape(1, d)

    convw = dn_conv[layer]
    alog = jnp.concatenate([dn_a_log_f[layer], dn_a_log_b[layer]]).reshape(1, 2 * DN_HEADS)
    dtb = jnp.concatenate([dn_dt_bias_f[layer], dn_dt_bias_b[layer]]).reshape(1, 2 * DN_HEADS)

    tb_c = _pick_block(bsz * n_ctx, 512)
    _, kx, vx, qkvd_c, _, lc_c, _, _ = _inproj(ctxf, sh1, sc1, lambda i: bsz, gain_mix, wa, wd, wzl, wg, bg, tb=tb_c)
    zero_state = jnp.zeros((bsz, DN_HEADS, DN_HEAD_DIM, DN_HEAD_DIM), F32)
    _, _, st_f, st_b = _delta(qkvd_c, lc_c, convw, alog, dtb, zero_state, zero_state, batch=bsz)

    tb = _pick_block(seq, 512)
    cos, sin = _rope_tables(seq)
    qa, ka, va, qkvd, z, lc, g_attn, g_dn = _inproj(
        xf, sh1, sc1, lambda i: i // (seq // tb), gain_mix, wa, wd, wzl, wg, bg, cos, sin, tb=tb, pos_blocks=seq // tb)
    o_attn = _attention(qa, ka, va, kx, vx, attn_sink[layer].reshape(1, ATTN_HEADS), batch=bsz)
    o_f, o_b, _, _ = _delta(qkvd, lc, convw, alog, dtb, st_f, st_b, batch=bsz)
    x1 = _merge(xf, gt1, o_f, o_b, z, dn_norm[layer].reshape(1, DN_HEAD_DIM), o_attn, g_attn, g_dn,
                w_br_attn[layer].astype(BF16), w_br_dn[layer].astype(BF16), w_out[layer].astype(BF16),
                tb=tb, blocks_per_batch=seq // tb)

    tb_p = _pick_block(seq, 512)
    out = _peer(x1, sh2, sc2, gt2, norm_ffn[layer].reshape(1, d), final_norm.reshape(1, d),
                peer_wq[layer].T.astype(BF16),
                peer_keys[layer].reshape(2 * PEER_HEADS, PEER_KEYS, PEER_KEY_HALF).astype(BF16),
                peer_u[layer].astype(BF16), peer_v[layer].T.astype(BF16),
                tb=tb_p, et=2048, blocks_per_batch=seq // tb_p)
    return out.reshape(bsz, seq, d)
```

```python
import functools
import math

import jax
import jax.numpy as jnp
from jax import lax
from jax.experimental import pallas as pl
from jax.experimental.pallas import tpu as pltpu

F32 = jnp.float32
BF16 = jnp.bfloat16
HIGHEST = lax.Precision.HIGHEST

GRID_W = 64
ATTN_HEADS = 8
ATTN_KV_HEADS = 2
ATTN_HEAD_DIM = 64
ATTN_WINDOW = 128
ATTN_BLOCK = 128
ROPE_BASE = 10000.0
ROPE_PAIRS = ATTN_HEAD_DIM // 4
DN_HEADS = 4
DN_HEAD_DIM = 128
DN_CONV = 5
DN_CHUNK = 64
PEER_HEADS = 8
PEER_KEYS = 128
PEER_KEY_HALF = 64
PEER_TOPK = 16
RMS_EPS = 1e-6
L2_EPS = 1e-6
LOG2E = math.log2(math.e)
ATTN_Q = ATTN_HEADS * ATTN_HEAD_DIM
ATTN_KV = ATTN_KV_HEADS * ATTN_HEAD_DIM
DN_WIDTH = DN_HEADS * DN_HEAD_DIM

LANES = 128
SUBLANES = 8
VMEM_LIMIT_BYTES = 56 * 1024 * 1024

MOD_ROWS = 16


def _dot(a, b):
    return jnp.dot(a.astype(BF16), b.astype(BF16), preferred_element_type=F32)


def _dot_nt(a, b):
    return lax.dot_general(a.astype(BF16), b.astype(BF16), (((1,), (1,)), ((), ())), preferred_element_type=F32)


def _dot_tn(a, b):
    return lax.dot_general(a.astype(BF16), b.astype(BF16), (((0,), (0,)), ((), ())), preferred_element_type=F32)


def _dot_hi(a, b):
    return jnp.dot(a, b, precision=HIGHEST, preferred_element_type=F32)


def _silu(x):
    return x * jax.nn.sigmoid(x)


def _softplus(x):
    return jnp.maximum(x, 0.0) + jnp.log1p(jnp.exp(-jnp.abs(x)))


def _params(*sem):
    return pltpu.CompilerParams(dimension_semantics=sem, vmem_limit_bytes=VMEM_LIMIT_BYTES)


def _ada_kernel(c_ref, w_ref, b_ref, o_ref):
    o_ref[...] = _dot_hi(_silu(c_ref[...]), w_ref[...]) + b_ref[...]


def _adaln(cc, w_ada, b_ada):
    d = cc.shape[1]
    n = w_ada.shape[1]
    tn = 512
    return pl.pallas_call(
        _ada_kernel,
        grid=(n // tn,),
        in_specs=[pl.BlockSpec((MOD_ROWS, d), lambda j: (0, 0)),
                  pl.BlockSpec((d, tn), lambda j: (0, j)),
                  pl.BlockSpec((1, tn), lambda j: (0, j))],
        out_specs=pl.BlockSpec((MOD_ROWS, tn), lambda j: (0, j)),
        out_shape=jax.ShapeDtypeStruct((MOD_ROWS, n), F32),
        compiler_params=_params("parallel"),
        name="adaln",
    )(cc, w_ada, b_ada.reshape(1, n))


def _rope(x, cos, sin):
    lane = lax.broadcasted_iota(jnp.int32, x.shape, 1)
    partner = jnp.where((lane % 32) < 16, pltpu.roll(x, LANES - 16, 1), pltpu.roll(x, 16, 1))
    return x * cos + partner * sin


def _inproj_kernel(*refs, rope):
    if rope:
        (x_ref, sh_ref, sc_ref, gain_ref, wa_ref, wd_ref, wzl_ref, wg_ref, bg_ref, cos_ref, sin_ref,
         qa_ref, ka_ref, va_ref, qkvd_ref, z_ref, lc_ref, ga_ref, gd_ref) = refs
    else:
        (x_ref, sh_ref, sc_ref, gain_ref, wa_ref, wd_ref, wzl_ref, wg_ref, bg_ref,
         qa_ref, ka_ref, va_ref, qkvd_ref, z_ref, lc_ref, ga_ref, gd_ref) = refs
    x = x_ref[...]
    y = x * lax.rsqrt(jnp.mean(x * x, axis=-1, keepdims=True) + RMS_EPS) * gain_ref[...]
    hb = (y * (1.0 + sc_ref[0]) + sh_ref[0]).astype(BF16)

    pa = jnp.dot(hb, wa_ref[...], preferred_element_type=F32)
    q = pa[:, :ATTN_Q] * (ATTN_HEAD_DIM ** -0.5 * LOG2E)
    k = pa[:, ATTN_Q:ATTN_Q + ATTN_KV]
    if rope:
        cos = cos_ref[...]
        sin = sin_ref[...]
        for j in range(ATTN_Q // LANES):
            qa_ref[:, j * LANES:(j + 1) * LANES] = _rope(q[:, j * LANES:(j + 1) * LANES], cos, sin).astype(BF16)
        ka_ref[...] = _rope(k, cos, sin).astype(BF16)
    else:
        qa_ref[...] = q.astype(BF16)
        ka_ref[...] = k.astype(BF16)
    va_ref[...] = pa[:, ATTN_Q + ATTN_KV:].astype(BF16)

    qkvd_ref[...] = jnp.dot(hb, wd_ref[...], preferred_element_type=F32)
    pzl = jnp.dot(hb, wzl_ref[...], preferred_element_type=F32)
    z_ref[...] = pzl[:, :DN_WIDTH].astype(z_ref.dtype)
    lc_ref[...] = pzl[:, DN_WIDTH:]
    gates = jax.nn.sigmoid(jnp.dot(hb, wg_ref[...], preferred_element_type=F32) + bg_ref[...])
    d = ga_ref.shape[1]
    ga_ref[...] = gates[:, :d].astype(ga_ref.dtype)
    gd_ref[...] = gates[:, d:].astype(gd_ref.dtype)


def _inproj(xf, sh, sc, mod_index, gain, wa, wd, wzl, wg, bg, cos=None, sin=None, *, tb, pos_blocks=1):
    t, d = xf.shape
    rope = cos is not None
    row = lambda i: (i, 0)
    const = lambda i: (0, 0)
    mod = lambda i: (mod_index(i), 0, 0)
    in_specs = [pl.BlockSpec((tb, d), row),
                pl.BlockSpec((1, 1, d), mod), pl.BlockSpec((1, 1, d), mod),
                pl.BlockSpec((1, d), const),
                pl.BlockSpec(wa.shape, const), pl.BlockSpec(wd.shape, const),
                pl.BlockSpec(wzl.shape, const), pl.BlockSpec(wg.shape, const), pl.BlockSpec(bg.shape, const)]
    args = [xf, sh, sc, gain, wa, wd, wzl, wg, bg]
    if rope:
        pos = lambda i: (i % pos_blocks, 0)
        in_specs += [pl.BlockSpec((tb, LANES), pos), pl.BlockSpec((tb, LANES), pos)]
        args += [cos, sin]
    widths = [(ATTN_Q, BF16), (ATTN_KV, BF16), (ATTN_KV, BF16), (3 * DN_WIDTH, F32), (DN_WIDTH, BF16),
              (4 * DN_HEADS, F32), (d, BF16), (d, BF16)]
    return pl.pallas_call(
        functools.partial(_inproj_kernel, rope=rope),
        grid=(t // tb,),
        in_specs=in_specs,
        out_specs=[pl.BlockSpec((tb, w), row) for w, _ in widths],
        out_shape=[jax.ShapeDtypeStruct((t, w), dt) for w, dt in widths],
        compiler_params=_params("parallel"),
        name="inproj_rope" if rope else "inproj",
    )(*args)


def _l2norm(x):
    return x * lax.rsqrt(jnp.sum(x * x, axis=-1, keepdims=True) + L2_EPS)


def _split_bf16(x):
    hi = x.astype(BF16).astype(F32)
    return hi, x - hi


def _dot_3pass(a, b):
    ah, al = _split_bf16(a)
    bh, bl = _split_bf16(b)
    return _dot(jnp.concatenate([ah, ah, al], axis=1), jnp.concatenate([bh, bl, bh], axis=0))


PREP_CHUNKS = 4
SCAN_CHUNKS = 4


def _delta_prep_kernel(cur_ref, prev_ref, next_ref, lc_ref, convw_ref, alog_ref, dtb_ref,
                       wq_f, u_f, ktt_f, att_f, gt_f, wq_b, u_b, ktt_b, att_b, gt_b, ext_ref, *, steps_per_seq):
    j = pl.program_id(0)
    c = DN_CHUNK
    rows = PREP_CHUNKS * c
    halo = SUBLANES - DN_CONV // 2
    first = (j % steps_per_seq) == 0
    last_step = (j % steps_per_seq) == steps_per_seq - 1
    ext_ref[0:SUBLANES, :] = prev_ref[...] * (1.0 - first.astype(F32))
    ext_ref[SUBLANES:SUBLANES + rows, :] = cur_ref[...]
    ext_ref[SUBLANES + rows:, :] = next_ref[...] * (1.0 - last_step.astype(F32))
    acc = ext_ref[pl.ds(halo, rows), :] * convw_ref[0:1, :]
    for tap in range(1, DN_CONV):
        acc = acc + ext_ref[pl.ds(halo + tap, rows), :] * convw_ref[tap:tap + 1, :]
    u = _silu(acc)

    lc = lc_ref[...]
    nd = 2 * DN_HEADS
    g_all = -jnp.exp(alog_ref[...]) * _softplus(lc[:, :nd] + dtb_ref[...])
    beta_all = jax.nn.sigmoid(lc[:, nd:])

    pair = 2 * c
    row = lax.broadcasted_iota(jnp.int32, (pair, pair), 0)
    col = lax.broadcasted_iota(jnp.int32, (pair, pair), 1)
    same_head = (row // c) == (col // c)
    eye = (row == col).astype(F32)
    row_c = lax.broadcasted_iota(jnp.int32, (c, c), 0)
    col_c = lax.broadcasted_iota(jnp.int32, (c, c), 1)
    lower_f = (row_c >= col_c).astype(F32)
    upper_f = (row_c <= col_c).astype(F32)
    is_fwd_col = lax.broadcasted_iota(jnp.int32, (c, nd), 1) < DN_HEADS
    first_head_lanes = lax.broadcasted_iota(jnp.int32, (c, pair), 1) < c

    out_refs = ((wq_f, u_f, ktt_f, att_f, gt_f), (wq_b, u_b, ktt_b, att_b, gt_b))
    chains = []
    for ci in range(PREP_CHUNKS):
        r0 = ci * c
        g = g_all[r0:r0 + c]
        cum = jnp.where(is_fwd_col, _dot_hi(lower_f, g), _dot_hi(upper_f, g))
        cum_t = cum.T
        for h0 in range(0, DN_HEADS, 2):
            def stacked(off):
                return jnp.concatenate([u[r0:r0 + c, off + h * DN_HEAD_DIM:off + (h + 1) * DN_HEAD_DIM]
                                        for h in (h0, h0 + 1)], axis=0)
            q = _l2norm(stacked(0)) * (DN_HEAD_DIM ** -0.5)
            k = _l2norm(stacked(DN_WIDTH))
            v = stacked(2 * DN_WIDTH)
            for d in range(2):
                reverse = d == 1
                cc = d * DN_HEADS + h0
                last = 0 if reverse else c - 1
                gcol = jnp.concatenate([cum[:, cc:cc + 1], cum[:, cc + 1:cc + 2]], axis=0)
                grow = jnp.concatenate([cum_t[cc:cc + 1, :], cum_t[cc + 1:cc + 2, :]], axis=1)
                glast_h = [cum[last:last + 1, cc + i:cc + i + 1] for i in range(2)]
                incl = same_head & ((row <= col) if reverse else (row >= col))
                chains.append(dict(
                    ci=ci, d=d, h0=h0, q=q, k=k, v=v, gcol=gcol, glast_h=glast_h,
                    glast=jnp.concatenate([jnp.broadcast_to(gl, (c, 1)) for gl in glast_h], axis=0),
                    bcol=jnp.concatenate([beta_all[r0:r0 + c, cc:cc + 1], beta_all[r0:r0 + c, cc + 1:cc + 2]], axis=0),
                    decay=jnp.exp(jnp.where(incl, gcol - grow, -jnp.inf)),
                    strict=same_head & ((row < col) if reverse else (row > col))))

    for ch in chains:
        ch["kb"] = ch["k"] * ch["bcol"]
        scores = _dot_nt(jnp.concatenate([ch["kb"], ch["q"]], axis=0), ch["k"])
        ch["x"] = -jnp.where(ch["strict"], scores[:pair] * ch["decay"], 0.0)
        ch["attn"] = scores[pair:] * ch["decay"]
    for ch in chains:
        ch["t"] = eye + ch["x"]
        ch["qpow"] = _dot_3pass(ch["x"], ch["x"])
    levels = int(math.log2(c)) - 1
    for lvl in range(levels):
        for ch in chains:
            if lvl < levels - 1:
                both = _dot_3pass(ch["qpow"], jnp.concatenate([ch["t"], ch["qpow"]], axis=1))
                ch["t"] = ch["t"] + both[:, :pair]
                ch["qpow"] = both[:, pair:]
            else:
                ch["t"] = ch["t"] + _dot_3pass(ch["qpow"], ch["t"])
    for ch in chains:
        eg = jnp.exp(ch["gcol"])
        ch["wu"] = _dot(ch["t"], jnp.concatenate([ch["kb"] * eg, ch["v"] * ch["bcol"]], axis=1))
        ch["qd"] = ch["q"] * eg
    for ch in chains:
        wq_ref, u_ref, ktt_ref, att_ref, gt_ref = out_refs[ch["d"]]
        ci, h0 = ch["ci"], ch["h0"]
        for i in range(2):
            head_rows = slice(i * c, (i + 1) * c)
            idx = ci * DN_HEADS + h0 + i
            wq_ref[idx, 0:c, :] = ch["wu"][head_rows, :DN_HEAD_DIM].astype(BF16)
            wq_ref[idx, c:, :] = ch["qd"][head_rows].astype(BF16)
            u_ref[ci * c:(ci + 1) * c, (h0 + i) * DN_HEAD_DIM:(h0 + i + 1) * DN_HEAD_DIM] = ch["wu"][head_rows, DN_HEAD_DIM:]
            gt_ref[idx] = jnp.broadcast_to(jnp.exp(ch["glast_h"][i]), gt_ref.shape[1:])
        k_tail = ch["k"] * jnp.exp(ch["glast"] - ch["gcol"])
        ktt_ref[ci * DN_HEAD_DIM:(ci + 1) * DN_HEAD_DIM, h0 * c:(h0 + 2) * c] = k_tail.T.astype(BF16)
        att_ref[ci * c:(ci + 1) * c, h0 * c:(h0 + 2) * c] = jnp.where(
            first_head_lanes, ch["attn"][:c], ch["attn"][c:]).astype(BF16)


def _delta_prep(qkvd, lc, convw, alog, dtb, *, batch):
    t, wd = qkvd.shape
    c = DN_CHUNK
    rows = PREP_CHUNKS * c
    n_steps = t // rows
    steps_per_seq = n_steps // batch
    sub_per_step = rows // SUBLANES
    last_sub = t // SUBLANES - 1
    n_chunks = t // c
    const = lambda j: (0, 0)
    out_specs = [pl.BlockSpec((PREP_CHUNKS * DN_HEADS, 2 * c, DN_HEAD_DIM), lambda j: (j, 0, 0)),
                 pl.BlockSpec((rows, DN_WIDTH), lambda j: (j, 0)),
                 pl.BlockSpec((PREP_CHUNKS * DN_HEAD_DIM, DN_HEADS * c), lambda j: (j, 0)),
                 pl.BlockSpec((rows, DN_HEADS * c), lambda j: (j, 0)),
                 pl.BlockSpec((PREP_CHUNKS * DN_HEADS, SUBLANES, LANES), lambda j: (j, 0, 0))]
    out_shape = [jax.ShapeDtypeStruct((n_chunks * DN_HEADS, 2 * c, DN_HEAD_DIM), BF16),
                 jax.ShapeDtypeStruct((t, DN_WIDTH), F32),
                 jax.ShapeDtypeStruct((n_chunks * DN_HEAD_DIM, DN_HEADS * c), BF16),
                 jax.ShapeDtypeStruct((t, DN_HEADS * c), BF16),
                 jax.ShapeDtypeStruct((n_chunks * DN_HEADS, SUBLANES, LANES), F32)]
    return pl.pallas_call(
        functools.partial(_delta_prep_kernel, steps_per_seq=steps_per_seq),
        grid=(n_steps,),
        in_specs=[pl.BlockSpec((rows, wd), lambda j: (j, 0)),
                  pl.BlockSpec((SUBLANES, wd), lambda j: (jnp.maximum(j * sub_per_step - 1, 0), 0)),
                  pl.BlockSpec((SUBLANES, wd), lambda j: (jnp.minimum((j + 1) * sub_per_step, last_sub), 0)),
                  pl.BlockSpec((rows, lc.shape[1]), lambda j: (j, 0)),
                  pl.BlockSpec(convw.shape, const), pl.BlockSpec(alog.shape, const), pl.BlockSpec(dtb.shape, const)],
        out_specs=out_specs + out_specs,
        out_shape=out_shape + out_shape,
        scratch_shapes=[pltpu.VMEM((rows + 2 * SUBLANES, wd), F32)],
        compiler_params=_params("parallel"),
        name="delta_prep",
    )(qkvd, qkvd, qkvd, lc, convw, alog, dtb)


def _delta_scan_kernel(wq_f, u_f, ktt_f, att_f, gt_f, wq_b, u_b, ktt_b, att_b, gt_b, s0f, s0b,
                       of_ref, ob_ref, sf_ref, sb_ref, state_f, state_b, *, n_steps):
    n = pl.program_id(1)
    c = DN_CHUNK

    @pl.when(n == 0)
    def _():
        state_f[...] = s0f[0]
        state_b[...] = s0b[0]

    mm = functools.partial(jnp.dot, preferred_element_type=F32)
    states = {(d, h): (state_f, state_b)[d][h] for d in range(2) for h in range(DN_HEADS)}
    for ci in range(SCAN_CHUNKS):
        chains = []
        for d, (refs, o_ref, cc) in enumerate((((wq_f, u_f, ktt_f, att_f, gt_f), of_ref, ci),
                                               ((wq_b, u_b, ktt_b, att_b, gt_b), ob_ref, SCAN_CHUNKS - 1 - ci))):
            for h in range(DN_HEADS):
                chains.append(dict(d=d, h=h, refs=refs, o_ref=o_ref, cc=cc, idx=cc * DN_HEADS + h,
                                   rows=slice(cc * c, (cc + 1) * c), hs=slice(h * DN_HEAD_DIM, (h + 1) * DN_HEAD_DIM),
                                   blk=slice(h * c, (h + 1) * c)))
        for ch in chains:
            wq_ref = ch["refs"][0]
            ch["both"] = mm(wq_ref[ch["idx"]], states[ch["d"], ch["h"]].astype(BF16))
        for ch in chains:
            u_ref = ch["refs"][1]
            ch["v_new"] = (u_ref[ch["rows"], ch["hs"]] - ch["both"][:c]).astype(BF16)
        for ch in chains:
            _, _, ktt_ref, att_ref, gt_ref = ch["refs"]
            cc = ch["cc"]
            o = ch["both"][c:] + mm(att_ref[ch["rows"], ch["blk"]], ch["v_new"])
            ch["o_ref"][ch["rows"], ch["hs"]] = o.astype(ch["o_ref"].dtype)
            grown = mm(ktt_ref[cc * DN_HEAD_DIM:(cc + 1) * DN_HEAD_DIM, ch["blk"]], ch["v_new"])
            s = states[ch["d"], ch["h"]]
            decayed = s.reshape(DN_HEAD_DIM // SUBLANES, SUBLANES, DN_HEAD_DIM) * gt_ref[ch["idx"]][None]
            states[ch["d"], ch["h"]] = decayed.reshape(DN_HEAD_DIM, DN_HEAD_DIM) + grown
    for d in range(2):
        for h in range(DN_HEADS):
            (state_f, state_b)[d][h] = states[d, h]

    @pl.when(n == n_steps - 1)
    def _():
        sf_ref[0] = state_f[...]
        sb_ref[0] = state_b[...]


def _delta_scan(prep, s0f, s0b, *, batch):
    t = prep[1].shape[0]
    c = DN_CHUNK
    rows = SCAN_CHUNKS * c
    n_steps = t // batch // rows
    fwd = lambda b, n: b * n_steps + n
    bwd = lambda b, n: b * n_steps + n_steps - 1 - n

    def specs(idx):
        return [pl.BlockSpec((SCAN_CHUNKS * DN_HEADS, 2 * c, DN_HEAD_DIM), lambda b, n: (idx(b, n), 0, 0)),
                pl.BlockSpec((rows, DN_WIDTH), lambda b, n: (idx(b, n), 0)),
                pl.BlockSpec((SCAN_CHUNKS * DN_HEAD_DIM, DN_HEADS * c), lambda b, n: (idx(b, n), 0)),
                pl.BlockSpec((rows, DN_HEADS * c), lambda b, n: (idx(b, n), 0)),
                pl.BlockSpec((SCAN_CHUNKS * DN_HEADS, SUBLANES, LANES), lambda b, n: (idx(b, n), 0, 0))]

    state_spec = pl.BlockSpec((1, DN_HEADS, DN_HEAD_DIM, DN_HEAD_DIM), lambda b, n: (b, 0, 0, 0))
    state_shape = jax.ShapeDtypeStruct((batch, DN_HEADS, DN_HEAD_DIM, DN_HEAD_DIM), F32)
    return pl.pallas_call(
        functools.partial(_delta_scan_kernel, n_steps=n_steps),
        grid=(batch, n_steps),
        in_specs=specs(fwd) + specs(bwd) + [state_spec, state_spec],
        out_specs=[pl.BlockSpec((rows, DN_WIDTH), lambda b, n: (fwd(b, n), 0)),
                   pl.BlockSpec((rows, DN_WIDTH), lambda b, n: (bwd(b, n), 0)),
                   state_spec, state_spec],
        out_shape=[jax.ShapeDtypeStruct((t, DN_WIDTH), BF16), jax.ShapeDtypeStruct((t, DN_WIDTH), BF16),
                   state_shape, state_shape],
        scratch_shapes=[pltpu.VMEM((DN_HEADS, DN_HEAD_DIM, DN_HEAD_DIM), F32),
                        pltpu.VMEM((DN_HEADS, DN_HEAD_DIM, DN_HEAD_DIM), F32)],
        compiler_params=_params("parallel", "arbitrary"),
        name="delta_scan",
    )(*prep, s0f, s0b)


def _delta(qkvd, lc, convw, alog, dtb, s0f, s0b, *, batch):
    return _delta_scan(_delta_prep(qkvd, lc, convw, alog, dtb, batch=batch), s0f, s0b, batch=batch)


def _attn_kernel(q_ref, kp_ref, kc_ref, kn_ref, vp_ref, vc_ref, vn_ref, kx_ref, vx_ref, sink_ref, o_ref, *, n_blocks):
    i = pl.program_id(1)
    qb = ATTN_BLOCK
    span = qb + 2 * ATTN_WINDOW
    n_ctx = kx_ref.shape[0]
    qpos = lax.broadcasted_iota(jnp.int32, (qb, span + n_ctx), 0)
    j = lax.broadcasted_iota(jnp.int32, (qb, span + n_ctx), 1)
    rel = j - qpos
    local = (rel >= 0) & (rel <= 2 * ATTN_WINDOW)
    local = local & ((j >= ATTN_WINDOW) | (i > 0)) & ((j < ATTN_WINDOW + qb) | (i < n_blocks - 1))
    valid = local | (j >= span)

    group = ATTN_HEADS // ATTN_KV_HEADS
    hd = ATTN_HEAD_DIM
    keys, vals = [], []
    for kv in range(ATTN_KV_HEADS):
        sl = slice(kv * hd, (kv + 1) * hd)
        keys.append(jnp.concatenate([kp_ref[:, sl], kc_ref[:, sl], kn_ref[:, sl], kx_ref[:, sl]], axis=0))
        vals.append(jnp.concatenate([vp_ref[:, sl], vc_ref[:, sl], vn_ref[:, sl], vx_ref[:, sl]], axis=0))
    scores = [_dot_nt(q_ref[:, h * hd:(h + 1) * hd], keys[h // group]) for h in range(ATTN_HEADS)]
    probs, inv_denoms = [], []
    for h in range(ATTN_HEADS):
        s = jnp.where(valid, scores[h], -jnp.inf)
        sink = sink_ref[0:1, h:h + 1] * LOG2E
        m = jnp.maximum(jnp.max(s, axis=-1, keepdims=True), sink)
        p = jnp.exp2(s - m)
        inv_denoms.append(1.0 / (jnp.sum(p, axis=-1, keepdims=True) + jnp.exp2(sink - m)))
        probs.append(p.astype(BF16))
    for h in range(ATTN_HEADS):
        o = jnp.dot(probs[h], vals[h // group], preferred_element_type=F32) * inv_denoms[h]
        o_ref[:, h * hd:(h + 1) * hd] = o.astype(o_ref.dtype)


def _attention(qa, ka, va, kx, vx, sink, *, batch):
    t = qa.shape[0]
    qb = ATTN_BLOCK
    n_blocks = t // batch // qb
    n_ctx = kx.shape[0] // batch
    cur = lambda b, i: (b * n_blocks + i, 0)
    prev = lambda b, i: (b * n_blocks + jnp.maximum(i - 1, 0), 0)
    nxt = lambda b, i: (b * n_blocks + jnp.minimum(i + 1, n_blocks - 1), 0)
    ctx = lambda b, i: (b, 0)
    kv_specs = [pl.BlockSpec((qb, ATTN_KV), prev), pl.BlockSpec((qb, ATTN_KV), cur), pl.BlockSpec((qb, ATTN_KV), nxt)]
    return pl.pallas_call(
        functools.partial(_attn_kernel, n_blocks=n_blocks),
        grid=(batch, n_blocks),
        in_specs=[pl.BlockSpec((qb, ATTN_Q), cur)] + kv_specs + kv_specs + [
            pl.BlockSpec((n_ctx, ATTN_KV), ctx), pl.BlockSpec((n_ctx, ATTN_KV), ctx),
            pl.BlockSpec(sink.shape, lambda b, i: (0, 0))],
        out_specs=pl.BlockSpec((qb, ATTN_Q), cur),
        out_shape=jax.ShapeDtypeStruct((t, ATTN_Q), BF16),
        compiler_params=_params("parallel", "parallel"),
        name="attention",
    )(qa, ka, ka, ka, va, va, va, kx, vx, sink)


def _merge_kernel(x_ref, gt_ref, of_ref, ob_ref, z_ref, dng_ref, oa_ref, ga_ref, gd_ref, wba_ref, wbd_ref, wo_ref, x1_ref):
    o = of_ref[...].astype(F32) + ob_ref[...].astype(F32)
    z = z_ref[...].astype(F32)
    heads = []
    for h in range(DN_HEADS):
        sl = slice(h * DN_HEAD_DIM, (h + 1) * DN_HEAD_DIM)
        oh = o[:, sl]
        yh = oh * lax.rsqrt(jnp.mean(oh * oh, axis=-1, keepdims=True) + RMS_EPS) * dng_ref[...]
        heads.append((yh * _silu(z[:, sl])).astype(BF16))
    o_dn = jnp.concatenate(heads, axis=1)
    y = (ga_ref[...] * jnp.dot(oa_ref[...], wba_ref[...], preferred_element_type=F32)
         + gd_ref[...] * jnp.dot(o_dn, wbd_ref[...], preferred_element_type=F32))
    x1_ref[...] = x_ref[...] + gt_ref[0] * jnp.dot(y.astype(BF16), wo_ref[...], preferred_element_type=F32)


def _merge(xf, gt1, o_f, o_b, z, dn_gain, o_attn, g_attn, g_dn, wba, wbd, wo, *, tb, blocks_per_batch):
    t, d = xf.shape
    row = lambda i: (i, 0)
    const = lambda i: (0, 0)
    rows = lambda w: pl.BlockSpec((tb, w), row)
    return pl.pallas_call(
        _merge_kernel,
        grid=(t // tb,),
        in_specs=[rows(d), pl.BlockSpec((1, 1, d), lambda i: (i // blocks_per_batch, 0, 0)),
                  rows(DN_WIDTH), rows(DN_WIDTH), rows(DN_WIDTH), pl.BlockSpec(dn_gain.shape, const),
                  rows(ATTN_Q), rows(d), rows(d),
                  pl.BlockSpec(wba.shape, const), pl.BlockSpec(wbd.shape, const), pl.BlockSpec(wo.shape, const)],
        out_specs=rows(d),
        out_shape=jax.ShapeDtypeStruct((t, d), F32),
        compiler_params=_params("parallel"),
        name="merge",
    )(xf, gt1, o_f, o_b, z, dn_gain, o_attn, g_attn, g_dn, wba, wbd, wo)


GATE_DTYPE = BF16
GELU_C = math.sqrt(2.0 / math.pi)


def _top_rows(x, k):
    rows = []
    for _ in range(k):
        m = jnp.max(x, axis=0, keepdims=True)
        rows.append(m)
        x = jnp.where(x == m, -jnp.inf, x)
    return rows


def _peer_kernel(x1_ref, sh_ref, sc_ref, gt_ref, gain_ref, fgain_ref, wqt_ref, keys_ref, u_ref, vt_ref, o_ref,
                 h_scr, thr_scr, a_scr, b_scr, act_scr, wa_scr, acc_scr, *, n_tiles):
    e = pl.program_id(1)
    tile_rows = u_ref.shape[0] // PEER_KEYS
    k = PEER_TOPK
    tb = x1_ref.shape[0]

    @pl.when(e == 0)
    def _():
        x1 = x1_ref[...]
        y = x1 * lax.rsqrt(jnp.mean(x1 * x1, axis=-1, keepdims=True) + RMS_EPS) * gain_ref[...]
        ht = (y * (1.0 + sc_ref[0]) + sh_ref[0]).T.astype(BF16)
        h_scr[...] = ht
        qt = jnp.dot(wqt_ref[...], ht, preferred_element_type=F32)
        for h in range(PEER_HEADS):
            halves = []
            for p in range(2):
                r0 = (2 * h + p) * PEER_KEY_HALF
                halves.append(jnp.dot(keys_ref[2 * h + p], qt[r0:r0 + PEER_KEY_HALF].astype(BF16),
                                      preferred_element_type=F32))
            s1, s2 = halves
            top1 = _top_rows(s1, k)
            top2 = _top_rows(s2, k)
            top2_mat = jnp.concatenate(top2, axis=0)
            cand = jnp.concatenate(
                [top1[0] + top2_mat]
                + [top1[r] + top2_mat[:k // 2] for r in range(1, k // 2)]
                + [jnp.concatenate(top1[k // 2:], axis=0) + top2[0]], axis=0)
            best = _top_rows(cand, k)
            zsum = jnp.zeros_like(best[0])
            for r in best:
                zsum = zsum + jnp.exp(r - best[0])
            thr_scr[h] = jnp.exp(best[-1] - s1 - top2[0])
            a_scr[h] = (0.5 * jnp.exp(s1 - top1[0])) / zsum
            b_scr[h] = jnp.exp(s2 - top2[0]).astype(GATE_DTYPE)
        acc_scr[...] = jnp.zeros_like(acc_scr)

    zero = jnp.zeros((), GATE_DTYPE)

    def gate(ii, cols):
        i = e * tile_rows + ii
        rows = slice(ii * PEER_KEYS, (ii + 1) * PEER_KEYS)
        w = None
        for h in range(PEER_HEADS):
            thr = thr_scr[h, pl.ds(i, 1), cols].astype(GATE_DTYPE)
            b = b_scr[h, :, cols]
            term = a_scr[h, pl.ds(i, 1), cols].astype(GATE_DTYPE) * jnp.where(b >= thr, b, zero)
            w = term if w is None else w + term
        x = act_scr[rows, cols].astype(GATE_DTYPE)
        wx = w * x
        inner = x * (GELU_C + (0.044715 * GELU_C) * (x * x))
        wa_scr[rows, cols] = (wx + wx * jnp.tanh(inner)).astype(BF16)

    n_split = 2
    width = tb // n_split
    halves =[slice(n * width, (n + 1) * width) for n in range(n_split)]
    for cols in halves:
        act_scr[:, cols] = jnp.dot(u_ref[...], h_scr[:, cols], preferred_element_type=F32)
    for cols in halves:
        for ii in range(tile_rows):
            gate(ii, cols)
        acc_scr[:, cols] += jnp.dot(vt_ref[...], wa_scr[:, cols], preferred_element_type=F32)

    @pl.when(e == n_tiles - 1)
    def _():
        y = x1_ref[...] + gt_ref[0] * acc_scr[...].T
        o_ref[...] = y * lax.rsqrt(jnp.mean(y * y, axis=-1, keepdims=True) + RMS_EPS) * fgain_ref[...]


def _peer(x1, sh2, sc2, gt2, gain, fgain, wqt, keys, u, vt, *, tb, et, blocks_per_batch):
    t, d = x1.shape
    n_exp = u.shape[0]
    n_tiles = n_exp // et
    row = lambda i, e: (i, 0)
    const2 = lambda i, e: (0, 0)
    mod = lambda i, e: (i // blocks_per_batch, 0, 0)
    kt = (PEER_HEADS, PEER_KEYS, tb)
    return pl.pallas_call(
        functools.partial(_peer_kernel, n_tiles=n_tiles),
        grid=(t // tb, n_tiles),
        in_specs=[pl.BlockSpec((tb, d), row),
                  pl.BlockSpec((1, 1, d), mod), pl.BlockSpec((1, 1, d), mod), pl.BlockSpec((1, 1, d), mod),
                  pl.BlockSpec((1, d), const2), pl.BlockSpec((1, d), const2),
                  pl.BlockSpec(wqt.shape, const2), pl.BlockSpec(keys.shape, lambda i, e: (0, 0, 0)),
                  pl.BlockSpec((et, d), lambda i, e: (e, 0)),
                  pl.BlockSpec((d, et), lambda i, e: (0, e))],
        out_specs=pl.BlockSpec((tb, d), row),
        out_shape=jax.ShapeDtypeStruct((t, d), F32),
        scratch_shapes=[pltpu.VMEM((d, tb), BF16),
                        pltpu.VMEM(kt, F32), pltpu.VMEM(kt, F32), pltpu.VMEM(kt, GATE_DTYPE),
                        pltpu.VMEM((et, tb), F32), pltpu.VMEM((et, tb), BF16), pltpu.VMEM((d, tb), F32)],
        compiler_params=_params("parallel", "arbitrary"),
        name="peer",
    )(x1, sh2, sc2, gt2, gain, fgain, wqt, keys, u, vt)


def _rope_tables(seq):
    t = jnp.arange(seq, dtype=jnp.int32)
    inv_freq = ROPE_BASE ** (-jnp.arange(ROPE_PAIRS, dtype=F32) / ROPE_PAIRS)
    ang_r = (t // GRID_W).astype(F32)[:, None] * inv_freq[None, :]
    ang_c = (t % GRID_W).astype(F32)[:, None] * inv_freq[None, :]
    cos_h = jnp.concatenate([jnp.cos(ang_r), jnp.cos(ang_r), jnp.cos(ang_c), jnp.cos(ang_c)], axis=1)
    sin_h = jnp.concatenate([-jnp.sin(ang_r), jnp.sin(ang_r), -jnp.sin(ang_c), jnp.sin(ang_c)], axis=1)
    reps = LANES // ATTN_HEAD_DIM
    return jnp.tile(cos_h, (1, reps)), jnp.tile(sin_h, (1, reps))


def _pick_block(n, target):
    b = min(n, target)
    while n % b:
        b //= 2
    return b


def kernel(x, c, ctx, c_ctx, w_ada, b_ada, norm_mix, norm_ffn, w_in, b_gate, attn_sink, dn_conv, dn_a_log_f, dn_dt_bias_f, dn_a_log_b, dn_dt_bias_b, dn_norm, w_br_attn, w_br_dn, w_out, peer_wq, peer_keys, peer_u, peer_v, final_norm):
    bsz, seq, d = x.shape
    n_ctx = ctx.shape[1]
    depth = w_ada.shape[0]
    assert depth == 1, "context-stream update between layers is not implemented"
    assert bsz + 1 <= MOD_ROWS and seq % ATTN_BLOCK == 0
    for length in (seq, n_ctx):
        assert length % (DN_CHUNK * PREP_CHUNKS) == 0 and length % (DN_CHUNK * SCAN_CHUNKS) == 0
    layer = 0
    xf = x.reshape(bsz * seq, d)
    ctxf = ctx.reshape(bsz * n_ctx, d)

    cc = jnp.zeros((MOD_ROWS, d), F32).at[:bsz].set(c).at[bsz].set(c_ctx)
    mod = _adaln(cc, w_ada[layer], b_ada[layer])
    sh1, sc1, gt1, sh2, sc2, gt2 = [m.reshape(MOD_ROWS, 1, d) for m in jnp.split(mod, 6, axis=-1)]

    w = w_in[layer].astype(BF16)
    o_qkv = ATTN_Q + 2 * ATTN_KV
    o_d = o_qkv + 3 * DN_WIDTH
    o_zl = o_d + DN_WIDTH + 4 * DN_HEADS
    wa, wd, wzl, wg = w[:, :o_qkv], w[:, o_qkv:o_d], w[:, o_d:o_zl], w[:, o_zl:]
    bg = b_gate[layer].reshape(1, -1)
    gain_mix = norm_mix[layer].reshape(1, d)

    convw = dn_conv[layer]
    alog = jnp.concatenate([dn_a_log_f[layer], dn_a_log_b[layer]]).reshape(1, 2 * DN_HEADS)
    dtb = jnp.concatenate([dn_dt_bias_f[layer], dn_dt_bias_b[layer]]).reshape(1, 2 * DN_HEADS)

    tb_c = _pick_block(bsz * n_ctx, 512)
    _, kx, vx, qkvd_c, _, lc_c, _, _ = _inproj(ctxf, sh1, sc1, lambda i: bsz, gain_mix, wa, wd, wzl, wg, bg, tb=tb_c)
    zero_state = jnp.zeros((bsz, DN_HEADS, DN_HEAD_DIM, DN_HEAD_DIM), F32)
    _, _, st_f, st_b = _delta(qkvd_c, lc_c, convw, alog, dtb, zero_state, zero_state, batch=bsz)

    tb = _pick_block(seq, 512)
    cos, sin = _rope_tables(seq)
    qa, ka, va, qkvd, z, lc, g_attn, g_dn = _inproj(
        xf, sh1, sc1, lambda i: i // (seq // tb), gain_mix, wa, wd, wzl, wg, bg, cos, sin, tb=tb, pos_blocks=seq // tb)
    o_attn = _attention(qa, ka, va, kx, vx, attn_sink[layer].reshape(1, ATTN_HEADS), batch=bsz)
    o_f, o_b, _, _ = _delta(qkvd, lc, convw, alog, dtb, st_f, st_b, batch=bsz)
    x1 = _merge(xf, gt1, o_f, o_b, z, dn_norm[layer].reshape(1, DN_HEAD_DIM), o_attn, g_attn, g_dn,
                w_br_attn[layer].astype(BF16), w_br_dn[layer].astype(BF16), w_out[layer].astype(BF16),
                tb=tb, blocks_per_batch=seq // tb)

    tb_p = _pick_block(seq, 512)
    out = _peer(x1, sh2, sc2, gt2, norm_ffn[layer].reshape(1, d), final_norm.reshape(1, d),
                peer_wq[layer].T.astype(BF16),
                peer_keys[layer].reshape(2 * PEER_HEADS, PEER_KEYS, PEER_KEY_HALF).astype(BF16),
                peer_u[layer].astype(BF16), peer_v[layer].T.astype(BF16),
                tb=tb_p, et=2048, blocks_per_batch=seq // tb_p)
    return out.reshape(bsz, seq, d)
```

```python
import functools
import math

import jax
import jax.numpy as jnp
from jax import lax
from jax.experimental import pallas as pl
from jax.experimental.pallas import tpu as pltpu

F32 = jnp.float32
BF16 = jnp.bfloat16
HIGHEST = lax.Precision.HIGHEST

GRID_W = 64
ATTN_HEADS = 8
ATTN_KV_HEADS = 2
ATTN_HEAD_DIM = 64
ATTN_WINDOW = 128
ATTN_BLOCK = 128
ROPE_BASE = 10000.0
ROPE_PAIRS = ATTN_HEAD_DIM // 4
DN_HEADS = 4
DN_HEAD_DIM = 128
DN_CONV = 5
DN_CHUNK = 64
PEER_HEADS = 8
PEER_KEYS = 128
PEER_KEY_HALF = 64
PEER_TOPK = 16
RMS_EPS = 1e-6
L2_EPS = 1e-6
LOG2E = math.log2(math.e)
ATTN_Q = ATTN_HEADS * ATTN_HEAD_DIM
ATTN_KV = ATTN_KV_HEADS * ATTN_HEAD_DIM
DN_WIDTH = DN_HEADS * DN_HEAD_DIM

LANES = 128
SUBLANES = 8
VMEM_LIMIT_BYTES = 56 * 1024 * 1024

MOD_ROWS = 16


def _dot(a, b):
    return jnp.dot(a.astype(BF16), b.astype(BF16), preferred_element_type=F32)


def _dot_nt(a, b):
    return lax.dot_general(a.astype(BF16), b.astype(BF16), (((1,), (1,)), ((), ())), preferred_element_type=F32)


def _dot_tn(a, b):
    return lax.dot_general(a.astype(BF16), b.astype(BF16), (((0,), (0,)), ((), ())), preferred_element_type=F32)


def _dot_hi(a, b):
    return jnp.dot(a, b, precision=HIGHEST, preferred_element_type=F32)


def _silu(x):
    return x * jax.nn.sigmoid(x)


def _softplus(x):
    return jnp.maximum(x, 0.0) + jnp.log1p(jnp.exp(-jnp.abs(x)))


def _params(*sem):
    return pltpu.CompilerParams(dimension_semantics=sem, vmem_limit_bytes=VMEM_LIMIT_BYTES)


def _ada_kernel(c_ref, w_ref, b_ref, o_ref):
    o_ref[...] = _dot_hi(_silu(c_ref[...]), w_ref[...]) + b_ref[...]


def _adaln(cc, w_ada, b_ada):
    d = cc.shape[1]
    n = w_ada.shape[1]
    tn = 512
    return pl.pallas_call(
        _ada_kernel,
        grid=(n // tn,),
        in_specs=[pl.BlockSpec((MOD_ROWS, d), lambda j: (0, 0)),
                  pl.BlockSpec((d, tn), lambda j: (0, j)),
                  pl.BlockSpec((1, tn), lambda j: (0, j))],
        out_specs=pl.BlockSpec((MOD_ROWS, tn), lambda j: (0, j)),
        out_shape=jax.ShapeDtypeStruct((MOD_ROWS, n), F32),
        compiler_params=_params("parallel"),
        name="adaln",
    )(cc, w_ada, b_ada.reshape(1, n))


def _rope(x, cos, sin):
    lane = lax.broadcasted_iota(jnp.int32, x.shape, 1)
    partner = jnp.where((lane % 32) < 16, pltpu.roll(x, LANES - 16, 1), pltpu.roll(x, 16, 1))
    return x * cos + partner * sin


def _inproj_kernel(*refs, rope):
    if rope:
        (x_ref, sh_ref, sc_ref, gain_ref, wa_ref, wd_ref, wzl_ref, wg_ref, bg_ref, cos_ref, sin_ref,
         qa_ref, ka_ref, va_ref, qkvd_ref, z_ref, lc_ref, ga_ref, gd_ref) = refs
    else:
        (x_ref, sh_ref, sc_ref, gain_ref, wa_ref, wd_ref, wzl_ref, wg_ref, bg_ref,
         qa_ref, ka_ref, va_ref, qkvd_ref, z_ref, lc_ref, ga_ref, gd_ref) = refs
    x = x_ref[...]
    y = x * lax.rsqrt(jnp.mean(x * x, axis=-1, keepdims=True) + RMS_EPS) * gain_ref[...]
    hb = (y * (1.0 + sc_ref[0]) + sh_ref[0]).astype(BF16)

    pa = jnp.dot(hb, wa_ref[...], preferred_element_type=F32)
    q = pa[:, :ATTN_Q] * (ATTN_HEAD_DIM ** -0.5 * LOG2E)
    k = pa[:, ATTN_Q:ATTN_Q + ATTN_KV]
    if rope:
        cos = cos_ref[...]
        sin = sin_ref[...]
        for j in range(ATTN_Q // LANES):
            qa_ref[:, j * LANES:(j + 1) * LANES] = _rope(q[:, j * LANES:(j + 1) * LANES], cos, sin).astype(BF16)
        ka_ref[...] = _rope(k, cos, sin).astype(BF16)
    else:
        qa_ref[...] = q.astype(BF16)
        ka_ref[...] = k.astype(BF16)
    va_ref[...] = pa[:, ATTN_Q + ATTN_KV:].astype(BF16)

    qkvd_ref[...] = jnp.dot(hb, wd_ref[...], preferred_element_type=F32)
    pzl = jnp.dot(hb, wzl_ref[...], preferred_element_type=F32)
    z_ref[...] = pzl[:, :DN_WIDTH].astype(z_ref.dtype)
    lc_ref[...] = pzl[:, DN_WIDTH:]
    gates = jax.nn.sigmoid(jnp.dot(hb, wg_ref[...], preferred_element_type=F32) + bg_ref[...])
    d = ga_ref.shape[1]
    ga_ref[...] = gates[:, :d].astype(ga_ref.dtype)
    gd_ref[...] = gates[:, d:].astype(gd_ref.dtype)


def _inproj(xf, sh, sc, mod_index, gain, wa, wd, wzl, wg, bg, cos=None, sin=None, *, tb, pos_blocks=1):
    t, d = xf.shape
    rope = cos is not None
    row = lambda i: (i, 0)
    const = lambda i: (0, 0)
    mod = lambda i: (mod_index(i), 0, 0)
    in_specs = [pl.BlockSpec((tb, d), row),
                pl.BlockSpec((1, 1, d), mod), pl.BlockSpec((1, 1, d), mod),
                pl.BlockSpec((1, d), const),
                pl.BlockSpec(wa.shape, const), pl.BlockSpec(wd.shape, const),
                pl.BlockSpec(wzl.shape, const), pl.BlockSpec(wg.shape, const), pl.BlockSpec(bg.shape, const)]
    args = [xf, sh, sc, gain, wa, wd, wzl, wg, bg]
    if rope:
        pos = lambda i: (i % pos_blocks, 0)
        in_specs += [pl.BlockSpec((tb, LANES), pos), pl.BlockSpec((tb, LANES), pos)]
        args += [cos, sin]
    widths = [(ATTN_Q, BF16), (ATTN_KV, BF16), (ATTN_KV, BF16), (3 * DN_WIDTH, F32), (DN_WIDTH, BF16),
              (4 * DN_HEADS, F32), (d, BF16), (d, BF16)]
    return pl.pallas_call(
        functools.partial(_inproj_kernel, rope=rope),
        grid=(t // tb,),
        in_specs=in_specs,
        out_specs=[pl.BlockSpec((tb, w), row) for w, _ in widths],
        out_shape=[jax.ShapeDtypeStruct((t, w), dt) for w, dt in widths],
        compiler_params=_params("parallel"),
        name="inproj_rope" if rope else "inproj",
    )(*args)


def _l2norm(x):
    return x * lax.rsqrt(jnp.sum(x * x, axis=-1, keepdims=True) + L2_EPS)


def _split_bf16(x):
    hi = x.astype(BF16).astype(F32)
    return hi, x - hi


def _dot_3pass(a, b):
    ah, al = _split_bf16(a)
    bh, bl = _split_bf16(b)
    return _dot(jnp.concatenate([ah, ah, al], axis=1), jnp.concatenate([bh, bl, bh], axis=0))


PREP_CHUNKS = 4
SCAN_CHUNKS = 4


def _delta_prep_kernel(cur_ref, prev_ref, next_ref, lc_ref, convw_ref, alog_ref, dtb_ref,
                       wq_f, u_f, ktt_f, att_f, gt_f, wq_b, u_b, ktt_b, att_b, gt_b, ext_ref, *, steps_per_seq):
    j = pl.program_id(0)
    c = DN_CHUNK
    rows = PREP_CHUNKS * c
    halo = SUBLANES - DN_CONV // 2
    first = (j % steps_per_seq) == 0
    last_step = (j % steps_per_seq) == steps_per_seq - 1
    ext_ref[0:SUBLANES, :] = prev_ref[...] * (1.0 - first.astype(F32))
    ext_ref[SUBLANES:SUBLANES + rows, :] = cur_ref[...]
    ext_ref[SUBLANES + rows:, :] = next_ref[...] * (1.0 - last_step.astype(F32))
    acc = ext_ref[pl.ds(halo, rows), :] * convw_ref[0:1, :]
    for tap in range(1, DN_CONV):
        acc = acc + ext_ref[pl.ds(halo + tap, rows), :] * convw_ref[tap:tap + 1, :]
    u = _silu(acc)

    lc = lc_ref[...]
    nd = 2 * DN_HEADS
    g_all = -jnp.exp(alog_ref[...]) * _softplus(lc[:, :nd] + dtb_ref[...])
    beta_all = jax.nn.sigmoid(lc[:, nd:])

    pair = 2 * c
    row = lax.broadcasted_iota(jnp.int32, (pair, pair), 0)
    col = lax.broadcasted_iota(jnp.int32, (pair, pair), 1)
    same_head = (row // c) == (col // c)
    eye = (row == col).astype(F32)
    row_c = lax.broadcasted_iota(jnp.int32, (c, c), 0)
    col_c = lax.broadcasted_iota(jnp.int32, (c, c), 1)
    lower_f = (row_c >= col_c).astype(F32)
    upper_f = (row_c <= col_c).astype(F32)
    is_fwd_col = lax.broadcasted_iota(jnp.int32, (c, nd), 1) < DN_HEADS
    first_head_lanes = lax.broadcasted_iota(jnp.int32, (c, pair), 1) < c

    out_refs = ((wq_f, u_f, ktt_f, att_f, gt_f), (wq_b, u_b, ktt_b, att_b, gt_b))
    chains = []
    for ci in range(PREP_CHUNKS):
        r0 = ci * c
        g = g_all[r0:r0 + c]
        cum = jnp.where(is_fwd_col, _dot_hi(lower_f, g), _dot_hi(upper_f, g))
        cum_t = cum.T
        for h0 in range(0, DN_HEADS, 2):
            def stacked(off):
                return jnp.concatenate([u[r0:r0 + c, off + h * DN_HEAD_DIM:off + (h + 1) * DN_HEAD_DIM]
                                        for h in (h0, h0 + 1)], axis=0)
            q = _l2norm(stacked(0)) * (DN_HEAD_DIM ** -0.5)
            k = _l2norm(stacked(DN_WIDTH))
            v = stacked(2 * DN_WIDTH)
            for d in range(2):
                reverse = d == 1
                cc = d * DN_HEADS + h0
                last = 0 if reverse else c - 1
                gcol = jnp.concatenate([cum[:, cc:cc + 1], cum[:, cc + 1:cc + 2]], axis=0)
                grow = jnp.concatenate([cum_t[cc:cc + 1, :], cum_t[cc + 1:cc + 2, :]], axis=1)
                glast_h = [cum[last:last + 1, cc + i:cc + i + 1] for i in range(2)]
                incl = same_head & ((row <= col) if reverse else (row >= col))
                chains.append(dict(
                    ci=ci, d=d, h0=h0, q=q, k=k, v=v, gcol=gcol, glast_h=glast_h,
                    glast=jnp.concatenate([jnp.broadcast_to(gl, (c, 1)) for gl in glast_h], axis=0),
                    bcol=jnp.concatenate([beta_all[r0:r0 + c, cc:cc + 1], beta_all[r0:r0 + c, cc + 1:cc + 2]], axis=0),
                    decay=jnp.exp(jnp.where(incl, gcol - grow, -jnp.inf)),
                    strict=same_head & ((row < col) if reverse else (row > col))))

    for ch in chains:
        ch["kb"] = ch["k"] * ch["bcol"]
        scores = _dot_nt(jnp.concatenate([ch["kb"], ch["q"]], axis=0), ch["k"])
        ch["x"] = -jnp.where(ch["strict"], scores[:pair] * ch["decay"], 0.0)
        ch["attn"] = scores[pair:] * ch["decay"]
    for ch in chains:
        ch["t"] = eye + ch["x"]
        ch["qpow"] = _dot_3pass(ch["x"], ch["x"])
    levels = int(math.log2(c)) - 1
    for lvl in range(levels):
        for ch in chains:
            if lvl < levels - 1:
                both = _dot_3pass(ch["qpow"], jnp.concatenate([ch["t"], ch["qpow"]], axis=1))
                ch["t"] = ch["t"] + both[:, :pair]
                ch["qpow"] = both[:, pair:]
            else:
                ch["t"] = ch["t"] + _dot_3pass(ch["qpow"], ch["t"])
    for ch in chains:
        eg = jnp.exp(ch["gcol"])
        ch["wu"] = _dot(ch["t"], jnp.concatenate([ch["kb"] * eg, ch["v"] * ch["bcol"]], axis=1))
        ch["qd"] = ch["q"] * eg
    for ch in chains:
        wq_ref, u_ref, ktt_ref, att_ref, gt_ref = out_refs[ch["d"]]
        ci, h0 = ch["ci"], ch["h0"]
        for i in range(2):
            head_rows = slice(i * c, (i + 1) * c)
            idx = ci * DN_HEADS + h0 + i
            wq_ref[idx, 0:c, :] = ch["wu"][head_rows, :DN_HEAD_DIM].astype(BF16)
            wq_ref[idx, c:, :] = ch["qd"][head_rows].astype(BF16)
            u_ref[ci * c:(ci + 1) * c, (h0 + i) * DN_HEAD_DIM:(h0 + i + 1) * DN_HEAD_DIM] = ch["wu"][head_rows, DN_HEAD_DIM:]
            gt_ref[idx] = jnp.broadcast_to(jnp.exp(ch["glast_h"][i]), gt_ref.shape[1:])
        k_tail = ch["k"] * jnp.exp(ch["glast"] - ch["gcol"])
        ktt_ref[ci * DN_HEAD_DIM:(ci + 1) * DN_HEAD_DIM, h0 * c:(h0 + 2) * c] = k_tail.T.astype(BF16)
        att_ref[ci * c:(ci + 1) * c, h0 * c:(h0 + 2) * c] = jnp.where(
            first_head_lanes, ch["attn"][:c], ch["attn"][c:]).astype(BF16)


def _delta_prep(qkvd, lc, convw, alog, dtb, *, batch):
    t, wd = qkvd.shape
    c = DN_CHUNK
    rows = PREP_CHUNKS * c
    n_steps = t // rows
    steps_per_seq = n_steps // batch
    sub_per_step = rows // SUBLANES
    last_sub = t // SUBLANES - 1
    n_chunks = t // c
    const = lambda j: (0, 0)
    out_specs = [pl.BlockSpec((PREP_CHUNKS * DN_HEADS, 2 * c, DN_HEAD_DIM), lambda j: (j, 0, 0)),
                 pl.BlockSpec((rows, DN_WIDTH), lambda j: (j, 0)),
                 pl.BlockSpec((PREP_CHUNKS * DN_HEAD_DIM, DN_HEADS * c), lambda j: (j, 0)),
                 pl.BlockSpec((rows, DN_HEADS * c), lambda j: (j, 0)),
                 pl.BlockSpec((PREP_CHUNKS * DN_HEADS, SUBLANES, LANES), lambda j: (j, 0, 0))]
    out_shape = [jax.ShapeDtypeStruct((n_chunks * DN_HEADS, 2 * c, DN_HEAD_DIM), BF16),
                 jax.ShapeDtypeStruct((t, DN_WIDTH), F32),
                 jax.ShapeDtypeStruct((n_chunks * DN_HEAD_DIM, DN_HEADS * c), BF16),
                 jax.ShapeDtypeStruct((t, DN_HEADS * c), BF16),
                 jax.ShapeDtypeStruct((n_chunks * DN_HEADS, SUBLANES, LANES), F32)]
    return pl.pallas_call(
        functools.partial(_delta_prep_kernel, steps_per_seq=steps_per_seq),
        grid=(n_steps,),
        in_specs=[pl.BlockSpec((rows, wd), lambda j: (j, 0)),
                  pl.BlockSpec((SUBLANES, wd), lambda j: (jnp.maximum(j * sub_per_step - 1, 0), 0)),
                  pl.BlockSpec((SUBLANES, wd), lambda j: (jnp.minimum((j + 1) * sub_per_step, last_sub), 0)),
                  pl.BlockSpec((rows, lc.shape[1]), lambda j: (j, 0)),
                  pl.BlockSpec(convw.shape, const), pl.BlockSpec(alog.shape, const), pl.BlockSpec(dtb.shape, const)],
        out_specs=out_specs + out_specs,
        out_shape=out_shape + out_shape,
        scratch_shapes=[pltpu.VMEM((rows + 2 * SUBLANES, wd), F32)],
        compiler_params=_params("parallel"),
        name="delta_prep",
    )(qkvd, qkvd, qkvd, lc, convw, alog, dtb)


def _delta_scan_kernel(wq_f, u_f, ktt_f, att_f, gt_f, wq_b, u_b, ktt_b, att_b, gt_b, s0f, s0b,
                       of_ref, ob_ref, sf_ref, sb_ref, state_f, state_b, *, n_steps):
    n = pl.program_id(1)
    c = DN_CHUNK

    @pl.when(n == 0)
    def _():
        state_f[...] = s0f[0]
        state_b[...] = s0b[0]

    mm = functools.partial(jnp.dot, preferred_element_type=F32)
    states = {(d, h): (state_f, state_b)[d][h] for d in range(2) for h in range(DN_HEADS)}
    for ci in range(SCAN_CHUNKS):
        chains = []
        for d, (refs, o_ref, cc) in enumerate((((wq_f, u_f, ktt_f, att_f, gt_f), of_ref, ci),
                                               ((wq_b, u_b, ktt_b, att_b, gt_b), ob_ref, SCAN_CHUNKS - 1 - ci))):
            for h in range(DN_HEADS):
                chains.append(dict(d=d, h=h, refs=refs, o_ref=o_ref, cc=cc, idx=cc * DN_HEADS + h,
                                   rows=slice(cc * c, (cc + 1) * c), hs=slice(h * DN_HEAD_DIM, (h + 1) * DN_HEAD_DIM),
                                   blk=slice(h * c, (h + 1) * c)))
        for ch in chains:
            wq_ref = ch["refs"][0]
            ch["both"] = mm(wq_ref[ch["idx"]], states[ch["d"], ch["h"]].astype(BF16))
        for ch in chains:
            u_ref = ch["refs"][1]
            ch["v_new"] = (u_ref[ch["rows"], ch["hs"]] - ch["both"][:c]).astype(BF16)
        for ch in chains:
            _, _, ktt_ref, att_ref, gt_ref = ch["refs"]
            cc = ch["cc"]
            o = ch["both"][c:] + mm(att_ref[ch["rows"], ch["blk"]], ch["v_new"])
            ch["o_ref"][ch["rows"], ch["hs"]] = o.astype(ch["o_ref"].dtype)
            grown = mm(ktt_ref[cc * DN_HEAD_DIM:(cc + 1) * DN_HEAD_DIM, ch["blk"]], ch["v_new"])
            s = states[ch["d"], ch["h"]]
            decayed = s.reshape(DN_HEAD_DIM // SUBLANES, SUBLANES, DN_HEAD_DIM) * gt_ref[ch["idx"]][None]
            states[ch["d"], ch["h"]] = decayed.reshape(DN_HEAD_DIM, DN_HEAD_DIM) + grown
    for d in range(2):
        for h in range(DN_HEADS):
            (state_f, state_b)[d][h] = states[d, h]

    @pl.when(n == n_steps - 1)
    def _():
        sf_ref[0] = state_f[...]
        sb_ref[0] = state_b[...]


def _delta_scan(prep, s0f, s0b, *, batch):
    t = prep[1].shape[0]
    c = DN_CHUNK
    rows = SCAN_CHUNKS * c
    n_steps = t // batch // rows
    fwd = lambda b, n: b * n_steps + n
    bwd = lambda b, n: b * n_steps + n_steps - 1 - n

    def specs(idx):
        return [pl.BlockSpec((SCAN_CHUNKS * DN_HEADS, 2 * c, DN_HEAD_DIM), lambda b, n: (idx(b, n), 0, 0)),
                pl.BlockSpec((rows, DN_WIDTH), lambda b, n: (idx(b, n), 0)),
                pl.BlockSpec((SCAN_CHUNKS * DN_HEAD_DIM, DN_HEADS * c), lambda b, n: (idx(b, n), 0)),
                pl.BlockSpec((rows, DN_HEADS * c), lambda b, n: (idx(b, n), 0)),
                pl.BlockSpec((SCAN_CHUNKS * DN_HEADS, SUBLANES, LANES), lambda b, n: (idx(b, n), 0, 0))]

    state_spec = pl.BlockSpec((1, DN_HEADS, DN_HEAD_DIM, DN_HEAD_DIM), lambda b, n: (b, 0, 0, 0))
    state_shape = jax.ShapeDtypeStruct((batch, DN_HEADS, DN_HEAD_DIM, DN_HEAD_DIM), F32)
    return pl.pallas_call(
        functools.partial(_delta_scan_kernel, n_steps=n_steps),
        grid=(batch, n_steps),
        in_specs=specs(fwd) + specs(bwd) + [state_spec, state_spec],
        out_specs=[pl.BlockSpec((rows, DN_WIDTH), lambda b, n: (fwd(b, n), 0)),
                   pl.BlockSpec((rows, DN_WIDTH), lambda b, n: (bwd(b, n), 0)),
                   state_spec, state_spec],
        out_shape=[jax.ShapeDtypeStruct((t, DN_WIDTH), BF16), jax.ShapeDtypeStruct((t, DN_WIDTH), BF16),
                   state_shape, state_shape],
        scratch_shapes=[pltpu.VMEM((DN_HEADS, DN_HEAD_DIM, DN_HEAD_DIM), F32),
                        pltpu.VMEM((DN_HEADS, DN_HEAD_DIM, DN_HEAD_DIM), F32)],
        compiler_params=_params("parallel", "arbitrary"),
        name="delta_scan",
    )(*prep, s0f, s0b)


def _delta(qkvd, lc, convw, alog, dtb, s0f, s0b, *, batch):
    return _delta_scan(_delta_prep(qkvd, lc, convw, alog, dtb, batch=batch), s0f, s0b, batch=batch)


def _attn_kernel(q_ref, kp_ref, kc_ref, kn_ref, vp_ref, vc_ref, vn_ref, kx_ref, vx_ref, sink_ref, o_ref, *, n_blocks):
    i = pl.program_id(1)
    qb = ATTN_BLOCK
    span = qb + 2 * ATTN_WINDOW
    n_ctx = kx_ref.shape[0]
    qpos = lax.broadcasted_iota(jnp.int32, (qb, span + n_ctx), 0)
    j = lax.broadcasted_iota(jnp.int32, (qb, span + n_ctx), 1)
    rel = j - qpos
    local = (rel >= 0) & (rel <= 2 * ATTN_WINDOW)
    local = local & ((j >= ATTN_WINDOW) | (i > 0)) & ((j < ATTN_WINDOW + qb) | (i < n_blocks - 1))
    valid = local | (j >= span)

    group = ATTN_HEADS // ATTN_KV_HEADS
    hd = ATTN_HEAD_DIM
    keys, vals = [], []
    for kv in range(ATTN_KV_HEADS):
        sl = slice(kv * hd, (kv + 1) * hd)
        keys.append(jnp.concatenate([kp_ref[:, sl], kc_ref[:, sl], kn_ref[:, sl], kx_ref[:, sl]], axis=0))
        vals.append(jnp.concatenate([vp_ref[:, sl], vc_ref[:, sl], vn_ref[:, sl], vx_ref[:, sl]], axis=0))
    scores = [_dot_nt(q_ref[:, h * hd:(h + 1) * hd], keys[h // group]) for h in range(ATTN_HEADS)]
    probs, inv_denoms = [], []
    for h in range(ATTN_HEADS):
        s = jnp.where(valid, scores[h], -jnp.inf)
        sink = sink_ref[0:1, h:h + 1] * LOG2E
        m = jnp.maximum(jnp.max(s, axis=-1, keepdims=True), sink)
        p = jnp.exp2(s - m)
        inv_denoms.append(1.0 / (jnp.sum(p, axis=-1, keepdims=True) + jnp.exp2(sink - m)))
        probs.append(p.astype(BF16))
    for h in range(ATTN_HEADS):
        o = jnp.dot(probs[h], vals[h // group], preferred_element_type=F32) * inv_denoms[h]
        o_ref[:, h * hd:(h + 1) * hd] = o.astype(o_ref.dtype)


def _attention(qa, ka, va, kx, vx, sink, *, batch):
    t = qa.shape[0]
    qb = ATTN_BLOCK
    n_blocks = t // batch // qb
    n_ctx = kx.shape[0] // batch
    cur = lambda b, i: (b * n_blocks + i, 0)
    prev = lambda b, i: (b * n_blocks + jnp.maximum(i - 1, 0), 0)
    nxt = lambda b, i: (b * n_blocks + jnp.minimum(i + 1, n_blocks - 1), 0)
    ctx = lambda b, i: (b, 0)
    kv_specs = [pl.BlockSpec((qb, ATTN_KV), prev), pl.BlockSpec((qb, ATTN_KV), cur), pl.BlockSpec((qb, ATTN_KV), nxt)]
    return pl.pallas_call(
        functools.partial(_attn_kernel, n_blocks=n_blocks),
        grid=(batch, n_blocks),
        in_specs=[pl.BlockSpec((qb, ATTN_Q), cur)] + kv_specs + kv_specs + [
            pl.BlockSpec((n_ctx, ATTN_KV), ctx), pl.BlockSpec((n_ctx, ATTN_KV), ctx),
            pl.BlockSpec(sink.shape, lambda b, i: (0, 0))],
        out_specs=pl.BlockSpec((qb, ATTN_Q), cur),
        out_shape=jax.ShapeDtypeStruct((t, ATTN_Q), BF16),
        compiler_params=_params("parallel", "parallel"),
        name="attention",
    )(qa, ka, ka, ka, va, va, va, kx, vx, sink)


def _merge_kernel(x_ref, gt_ref, of_ref, ob_ref, z_ref, dng_ref, oa_ref, ga_ref, gd_ref, wba_ref, wbd_ref, wo_ref, x1_ref):
    o = of_ref[...].astype(F32) + ob_ref[...].astype(F32)
    z = z_ref[...].astype(F32)
    heads = []
    for h in range(DN_HEADS):
        sl = slice(h * DN_HEAD_DIM, (h + 1) * DN_HEAD_DIM)
        oh = o[:, sl]
        yh = oh * lax.rsqrt(jnp.mean(oh * oh, axis=-1, keepdims=True) + RMS_EPS) * dng_ref[...]
        heads.append((yh * _silu(z[:, sl])).astype(BF16))
    o_dn = jnp.concatenate(heads, axis=1)
    y = (ga_ref[...] * jnp.dot(oa_ref[...], wba_ref[...], preferred_element_type=F32)
         + gd_ref[...] * jnp.dot(o_dn, wbd_ref[...], preferred_element_type=F32))
    x1_ref[...] = x_ref[...] + gt_ref[0] * jnp.dot(y.astype(BF16), wo_ref[...], preferred_element_type=F32)


def _merge(xf, gt1, o_f, o_b, z, dn_gain, o_attn, g_attn, g_dn, wba, wbd, wo, *, tb, blocks_per_batch):
    t, d = xf.shape
    row = lambda i: (i, 0)
    const = lambda i: (0, 0)
    rows = lambda w: pl.BlockSpec((tb, w), row)
    return pl.pallas_call(
        _merge_kernel,
        grid=(t // tb,),
        in_specs=[rows(d), pl.BlockSpec((1, 1, d), lambda i: (i // blocks_per_batch, 0, 0)),
                  rows(DN_WIDTH), rows(DN_WIDTH), rows(DN_WIDTH), pl.BlockSpec(dn_gain.shape, const),
                  rows(ATTN_Q), rows(d), rows(d),
                  pl.BlockSpec(wba.shape, const), pl.BlockSpec(wbd.shape, const), pl.BlockSpec(wo.shape, const)],
        out_specs=rows(d),
        out_shape=jax.ShapeDtypeStruct((t, d), F32),
        compiler_params=_params("parallel"),
        name="merge",
    )(xf, gt1, o_f, o_b, z, dn_gain, o_attn, g_attn, g_dn, wba, wbd, wo)


GATE_DTYPE = BF16
GELU_C = math.sqrt(2.0 / math.pi)


def _top_rows(x, k):
    rows = [jnp.max(x, axis=0, keepdims=True)]
    for _ in range(k - 1):
        rows.append(jnp.max(jnp.where(x < rows[-1], x, -jnp.inf), axis=0, keepdims=True))
    return rows


def _peer_kernel(x1_ref, sh_ref, sc_ref, gt_ref, gain_ref, fgain_ref, wqt_ref, keys_ref, u_ref, vt_ref, o_ref,
                 h_scr, thr_scr, a_scr, b_scr, act_scr, wa_scr, acc_scr, *, n_tiles):
    e = pl.program_id(1)
    tile_rows = u_ref.shape[0] // PEER_KEYS
    k = PEER_TOPK
    tb = x1_ref.shape[0]

    @pl.when(e == 0)
    def _():
        x1 = x1_ref[...]
        y = x1 * lax.rsqrt(jnp.mean(x1 * x1, axis=-1, keepdims=True) + RMS_EPS) * gain_ref[...]
        ht = (y * (1.0 + sc_ref[0]) + sh_ref[0]).T.astype(BF16)
        h_scr[...] = ht
        qt = jnp.dot(wqt_ref[...], ht, preferred_element_type=F32)
        for h in range(PEER_HEADS):
            halves = []
            for p in range(2):
                r0 = (2 * h + p) * PEER_KEY_HALF
                halves.append(jnp.dot(keys_ref[2 * h + p], qt[r0:r0 + PEER_KEY_HALF].astype(BF16),
                                      preferred_element_type=F32))
            s1, s2 = halves
            top1 = _top_rows(s1, k)
            top2 = _top_rows(s2, k)
            top2_mat = jnp.concatenate(top2, axis=0)
            cand = jnp.concatenate(
                [top1[0] + top2_mat]
                + [top1[r] + top2_mat[:k // 2] for r in range(1, k // 2)]
                + [jnp.concatenate(top1[k // 2:], axis=0) + top2[0]], axis=0)
            best = _top_rows(cand, k)
            zsum = jnp.zeros_like(best[0])
            for r in best:
                zsum = zsum + jnp.exp(r - best[0])
            thr_scr[h] = jnp.exp(best[-1] - s1 - top2[0])
            a_scr[h] = (0.5 * jnp.exp(s1 - top1[0])) / zsum
            b_scr[h] = jnp.exp(s2 - top2[0]).astype(GATE_DTYPE)
        acc_scr[...] = jnp.zeros_like(acc_scr)

    zero = jnp.zeros((), GATE_DTYPE)

    def gate(ii, cols):
        i = e * tile_rows + ii
        rows = slice(ii * PEER_KEYS, (ii + 1) * PEER_KEYS)
        w = None
        for h in range(PEER_HEADS):
            thr = thr_scr[h, pl.ds(i, 1), cols].astype(GATE_DTYPE)
            b = b_scr[h, :, cols]
            term = a_scr[h, pl.ds(i, 1), cols].astype(GATE_DTYPE) * jnp.where(b >= thr, b, zero)
            w = term if w is None else w + term
        x = act_scr[rows, cols].astype(GATE_DTYPE)
        wx = w * x
        inner = x * (GELU_C + (0.044715 * GELU_C) * (x * x))
        wa_scr[rows, cols] = (wx + wx * jnp.tanh(inner)).astype(BF16)

    n_split = 2
    width = tb // n_split
    halves =[slice(n * width, (n + 1) * width) for n in range(n_split)]
    for cols in halves:
        act_scr[:, cols] = jnp.dot(u_ref[...], h_scr[:, cols], preferred_element_type=F32)
    for cols in halves:
        for ii in range(tile_rows):
            gate(ii, cols)
        acc_scr[:, cols] += jnp.dot(vt_ref[...], wa_scr[:, cols], preferred_element_type=F32)

    @pl.when(e == n_tiles - 1)
    def _():
        y = x1_ref[...] + gt_ref[0] * acc_scr[...].T
        o_ref[...] = y * lax.rsqrt(jnp.mean(y * y, axis=-1, keepdims=True) + RMS_EPS) * fgain_ref[...]


def _peer(x1, sh2, sc2, gt2, gain, fgain, wqt, keys, u, vt, *, tb, et, blocks_per_batch):
    t, d = x1.shape
    n_exp = u.shape[0]
    n_tiles = n_exp // et
    row = lambda i, e: (i, 0)
    const2 = lambda i, e: (0, 0)
    mod = lambda i, e: (i // blocks_per_batch, 0, 0)
    kt = (PEER_HEADS, PEER_KEYS, tb)
    return pl.pallas_call(
        functools.partial(_peer_kernel, n_tiles=n_tiles),
        grid=(t // tb, n_tiles),
        in_specs=[pl.BlockSpec((tb, d), row),
                  pl.BlockSpec((1, 1, d), mod), pl.BlockSpec((1, 1, d), mod), pl.BlockSpec((1, 1, d), mod),
                  pl.BlockSpec((1, d), const2), pl.BlockSpec((1, d), const2),
                  pl.BlockSpec(wqt.shape, const2), pl.BlockSpec(keys.shape, lambda i, e: (0, 0, 0)),
                  pl.BlockSpec((et, d), lambda i, e: (e, 0)),
                  pl.BlockSpec((d, et), lambda i, e: (0, e))],
        out_specs=pl.BlockSpec((tb, d), row),
        out_shape=jax.ShapeDtypeStruct((t, d), F32),
        scratch_shapes=[pltpu.VMEM((d, tb), BF16),
                        pltpu.VMEM(kt, F32), pltpu.VMEM(kt, F32), pltpu.VMEM(kt, GATE_DTYPE),
                        pltpu.VMEM((et, tb), F32), pltpu.VMEM((et, tb), BF16), pltpu.VMEM((d, tb), F32)],
        compiler_params=_params("parallel", "arbitrary"),
        name="peer",
    )(x1, sh2, sc2, gt2, gain, fgain, wqt, keys, u, vt)


def _rope_tables(seq):
    t = jnp.arange(seq, dtype=jnp.int32)
    inv_freq = ROPE_BASE ** (-jnp.arange(ROPE_PAIRS, dtype=F32) / ROPE_PAIRS)
    ang_r = (t // GRID_W).astype(F32)[:, None] * inv_freq[None, :]
    ang_c = (t % GRID_W).astype(F32)[:, None] * inv_freq[None, :]
    cos_h = jnp.concatenate([jnp.cos(ang_r), jnp.cos(ang_r), jnp.cos(ang_c), jnp.cos(ang_c)], axis=1)
    sin_h = jnp.concatenate([-jnp.sin(ang_r), jnp.sin(ang_r), -jnp.sin(ang_c), jnp.sin(ang_c)], axis=1)
    reps = LANES // ATTN_HEAD_DIM
    return jnp.tile(cos_h, (1, reps)), jnp.tile(sin_h, (1, reps))


def _pick_block(n, target):
    b = min(n, target)
    while n % b:
        b //= 2
    return b


def kernel(x, c, ctx, c_ctx, w_ada, b_ada, norm_mix, norm_ffn, w_in, b_gate, attn_sink, dn_conv, dn_a_log_f, dn_dt_bias_f, dn_a_log_b, dn_dt_bias_b, dn_norm, w_br_attn, w_br_dn, w_out, peer_wq, peer_keys, peer_u, peer_v, final_norm):
    bsz, seq, d = x.shape
    n_ctx = ctx.shape[1]
    depth = w_ada.shape[0]
    assert depth == 1, "context-stream update between layers is not implemented"
    assert bsz + 1 <= MOD_ROWS and seq % ATTN_BLOCK == 0
    for length in (seq, n_ctx):
        assert length % (DN_CHUNK * PREP_CHUNKS) == 0 and length % (DN_CHUNK * SCAN_CHUNKS) == 0
    layer = 0
    xf = x.reshape(bsz * seq, d)
    ctxf = ctx.reshape(bsz * n_ctx, d)

    cc = jnp.zeros((MOD_ROWS, d), F32).at[:bsz].set(c).at[bsz].set(c_ctx)
    mod = _adaln(cc, w_ada[layer], b_ada[layer])
    sh1, sc1, gt1, sh2, sc2, gt2 = [m.reshape(MOD_ROWS, 1, d) for m in jnp.split(mod, 6, axis=-1)]

    w = w_in[layer].astype(BF16)
    o_qkv = ATTN_Q + 2 * ATTN_KV
    o_d = o_qkv + 3 * DN_WIDTH
    o_zl = o_d + DN_WIDTH + 4 * DN_HEADS
    wa, wd, wzl, wg = w[:, :o_qkv], w[:, o_qkv:o_d], w[:, o_d:o_zl], w[:, o_zl:]
    bg = b_gate[layer].reshape(1, -1)
    gain_mix = norm_mix[layer].reshape(1, d)

    convw = dn_conv[layer]
    alog = jnp.concatenate([dn_a_log_f[layer], dn_a_log_b[layer]]).reshape(1, 2 * DN_HEADS)
    dtb = jnp.concatenate([dn_dt_bias_f[layer], dn_dt_bias_b[layer]]).reshape(1, 2 * DN_HEADS)

    tb_c = _pick_block(bsz * n_ctx, 512)
    _, kx, vx, qkvd_c, _, lc_c, _, _ = _inproj(ctxf, sh1, sc1, lambda i: bsz, gain_mix, wa, wd, wzl, wg, bg, tb=tb_c)
    zero_state = jnp.zeros((bsz, DN_HEADS, DN_HEAD_DIM, DN_HEAD_DIM), F32)
    _, _, st_f, st_b = _delta(qkvd_c, lc_c, convw, alog, dtb, zero_state, zero_state, batch=bsz)

    tb = _pick_block(seq, 512)
    cos, sin = _rope_tables(seq)
    qa, ka, va, qkvd, z, lc, g_attn, g_dn = _inproj(
        xf, sh1, sc1, lambda i: i // (seq // tb), gain_mix, wa, wd, wzl, wg, bg, cos, sin, tb=tb, pos_blocks=seq // tb)
    o_attn = _attention(qa, ka, va, kx, vx, attn_sink[layer].reshape(1, ATTN_HEADS), batch=bsz)
    o_f, o_b, _, _ = _delta(qkvd, lc, convw, alog, dtb, st_f, st_b, batch=bsz)
    x1 = _merge(xf, gt1, o_f, o_b, z, dn_norm[layer].reshape(1, DN_HEAD_DIM), o_attn, g_attn, g_dn,
                w_br_attn[layer].astype(BF16), w_br_dn[layer].astype(BF16), w_out[layer].astype(BF16),
                tb=tb, blocks_per_batch=seq // tb)

    tb_p = _pick_block(seq, 512)
    out = _peer(x1, sh2, sc2, gt2, norm_ffn[layer].reshape(1, d), final_norm.reshape(1, d),
                peer_wq[layer].T.astype(BF16),
                peer_keys[layer].reshape(2 * PEER_HEADS, PEER_KEYS, PEER_KEY_HALF).astype(BF16),
                peer_u[layer].astype(BF16), peer_v[layer].T.astype(BF16),
                tb=tb_p, et=2048, blocks_per_batch=seq // tb_p)
    return out.reshape(bsz, seq, d)
```
